```python
import jax, jax.numpy as jnp
from jax import lax
import numpy as np

D_MODEL = 2048
BATCH = 4
SEQ = 4096
DEPTH = 4

N_MIXERS = 3
RMS_EPS = 1e-6
D_FF = 5504
N_HEADS = 16
HEAD_DIM = D_MODEL // N_HEADS
MOBA_BLOCK = 256
MOBA_TOPK = 3
MOBA_QCHUNK = 16
POOL_WINDOWS = (2, 4, 8, 16)
N_POOL_GROUPS = len(POOL_WINDOWS)
POOL_GROUP_DIM = D_MODEL // N_POOL_GROUPS
SGU_DIM = 3 * D_MODEL
SGU_CHUNK = 128
SGU_GROUPS = 8
SGU_GROUP_DIM = SGU_DIM // SGU_GROUPS
N_ATTN_LAYERS = (DEPTH + 2) // 3
N_POOL_LAYERS = (DEPTH + 1) // 3
N_SGU_LAYERS = DEPTH // 3

kernel_name = "interleaved_moba_pool_sgu_macaron"


def _rms(x, gain):
    x32 = x.astype(jnp.float32)
    y = x32 * lax.rsqrt(jnp.mean(x32 * x32, axis=-1, keepdims=True) + RMS_EPS)
    return (y * gain.astype(jnp.float32)).astype(x.dtype)


def _swiglu(h, w_gate, w_up, w_down):
    return (jax.nn.silu(h @ w_gate) * (h @ w_up)) @ w_down


def _alibi_slopes():
    return jnp.asarray(2.0 ** (-8.0 * np.arange(1, N_HEADS + 1) / N_HEADS), dtype=jnp.float32)


def _moba_attention(h, w_qkv, q_gain, k_gain, w_out):
    b, s, _ = h.shape
    qkv = (h @ w_qkv).reshape(b, s, 3, N_HEADS, HEAD_DIM)
    q = _rms(qkv[:, :, 0], q_gain)
    k = _rms(qkv[:, :, 1], k_gain)
    v = qkv[:, :, 2]
    n_blocks = -(-s // MOBA_BLOCK)
    s_pad = n_blocks * MOBA_BLOCK
    pad = ((0, 0), (0, s_pad - s), (0, 0), (0, 0))
    q, k, v = [jnp.pad(t, pad).transpose(0, 2, 1, 3) for t in (q, k, v)]
    kb = k.reshape(b, N_HEADS, n_blocks, MOBA_BLOCK, HEAD_DIM)
    vb = v.reshape(b, N_HEADS, n_blocks, MOBA_BLOCK, HEAD_DIM)
    k_mean = jnp.mean(kb.astype(jnp.float32), axis=3)
    topk = min(MOBA_TOPK, n_blocks)
    slopes = _alibi_slopes()
    scale = HEAD_DIM ** -0.5
    n_qc = s_pad // MOBA_QCHUNK
    q_chunks = q.reshape(b, N_HEADS, n_qc, MOBA_QCHUNK, HEAD_DIM).transpose(2, 0, 1, 3, 4)
    starts = jnp.arange(n_qc, dtype=jnp.int32) * MOBA_QCHUNK
    bi = jnp.arange(b)[:, None, None, None]
    hi = jnp.arange(N_HEADS)[None, :, None, None]
    offs = jnp.arange(MOBA_BLOCK, dtype=jnp.int32)
    block_ids = jnp.arange(n_blocks, dtype=jnp.int32)

    def one_chunk(args):
        qc, start = args
        qpos = start + jnp.arange(MOBA_QCHUNK, dtype=jnp.int32)
        own = start // MOBA_BLOCK
        qc32 = qc.astype(jnp.float32)
        gate = jnp.einsum('bhqd,bhnd->bhqn', qc32, k_mean)
        gate = jnp.where(block_ids < own, gate, -jnp.inf)
        _, sel = lax.top_k(gate, topk)
        valid = jnp.arange(topk) < own
        k_sel = kb[bi, hi, sel]
        v_sel = vb[bi, hi, sel]
        kpos_sel = sel[..., None] * MOBA_BLOCK + offs
        dist_sel = (qpos[:, None, None] - kpos_sel).astype(jnp.float32)
        lg_sel = jnp.einsum('bhqd,bhqrsd->bhqrs', qc32, k_sel.astype(jnp.float32)) * scale
        lg_sel = lg_sel - slopes[:, None, None, None] * dist_sel
        lg_sel = jnp.where(valid[:, None], lg_sel, -jnp.inf)
        lg_sel = lg_sel.reshape(b, N_HEADS, MOBA_QCHUNK, topk * MOBA_BLOCK)
        k_own = lax.dynamic_index_in_dim(kb, own, axis=2, keepdims=False)
        v_own = lax.dynamic_index_in_dim(vb, own, axis=2, keepdims=False)
        dist_own = (qpos[:, None] - (own * MOBA_BLOCK + offs)[None, :]).astype(jnp.float32)
        lg_own = jnp.einsum('bhqd,bhsd->bhqs', qc32, k_own.astype(jnp.float32)) * scale
        lg_own = jnp.where(dist_own >= 0, lg_own - slopes[:, None, None] * dist_own, -jnp.inf)
        p = jax.nn.softmax(jnp.concatenate([lg_sel, lg_own], axis=-1), axis=-1).astype(v.dtype)
        p_sel = p[..., :topk * MOBA_BLOCK].reshape(b, N_HEADS, MOBA_QCHUNK, topk, MOBA_BLOCK)
        p_own = p[..., topk * MOBA_BLOCK:]
        return (jnp.einsum('bhqrs,bhqrsd->bhqd', p_sel, v_sel)
                + jnp.einsum('bhqs,bhsd->bhqd', p_own, v_own))

    o = lax.map(one_chunk, (q_chunks, starts))
    o = o.transpose(1, 0, 3, 2, 4).reshape(b, s_pad, D_MODEL)[:, :s]
    return o @ w_out


def _pool_mixer(h, w_group, scale):
    b, s, _ = h.shape
    hg = h.reshape(b, s, N_POOL_GROUPS, POOL_GROUP_DIM)
    csum = jnp.cumsum(hg.astype(jnp.float32), axis=1)
    count = jnp.arange(1, s + 1, dtype=jnp.float32)
    means = []
    for g, w in enumerate(POOL_WINDOWS):
        c = csum[:, :, g]
        prev = jnp.pad(c[:, :-w], ((0, 0), (w, 0), (0, 0)))
        means.append((c - prev) / jnp.minimum(count, w)[None, :, None])
    pooled = jnp.stack(means, axis=2).astype(h.dtype) - hg
    y = jnp.einsum('bsgc,gcd->bsgd', pooled, w_group).reshape(b, s, D_MODEL)
    return y * scale


def _sgu_mixer(h, w_in, v_gain, w_spatial, b_spatial, w_out):
    b, s, _ = h.shape
    z = jax.nn.gelu(h @ w_in, approximate=False)
    u, v = jnp.split(z, 2, axis=-1)
    v = _rms(v, v_gain)
    n_c = s // SGU_CHUNK
    vc = v.reshape(b, n_c, SGU_CHUNK, SGU_GROUPS, SGU_GROUP_DIM)
    causal = jnp.tril(jnp.ones((SGU_CHUNK, SGU_CHUNK), dtype=bool))
    w_s = jnp.where(causal, w_spatial, 0)
    sv = jnp.einsum('gts,bcsgd->bctgd', w_s, vc) + b_spatial.T[None, None, :, :, None]
    return (u * sv.reshape(b, s, SGU_DIM)) @ w_out


def setup_inputs(seed: int = 0) -> dict:
    key = jax.random.key(seed)
    ks = jax.random.split(key, 24)
    f32 = jnp.float32

    def nrm(k, shape, fan_in, mult=1.0):
        return jax.random.normal(k, shape, f32) * (mult * fan_in ** -0.5)

    def gain(k, shape):
        return 1.0 + 0.1 * jax.random.normal(k, shape, f32)

    return {
        "x": jax.random.normal(ks[0], (BATCH, SEQ, D_MODEL), f32),
        "ffn1_norm": gain(ks[1], (DEPTH, D_MODEL)),
        "ffn1_w_gate": nrm(ks[2], (DEPTH, D_MODEL, D_FF), D_MODEL),
        "ffn1_w_up": nrm(ks[3], (DEPTH, D_MODEL, D_FF), D_MODEL),
        "ffn1_w_down": nrm(ks[4], (DEPTH, D_FF, D_MODEL), D_FF),
        "mix_norm": gain(ks[5], (DEPTH, D_MODEL)),
        "ffn2_norm": gain(ks[6], (DEPTH, D_MODEL)),
        "ffn2_w_gate": nrm(ks[7], (DEPTH, D_MODEL, D_FF), D_MODEL),
        "ffn2_w_up": nrm(ks[8], (DEPTH, D_MODEL, D_FF), D_MODEL),
        "ffn2_w_down": nrm(ks[9], (DEPTH, D_FF, D_MODEL), D_FF),
        "attn_w_qkv": nrm(ks[10], (N_ATTN_LAYERS, D_MODEL, 3 * D_MODEL), D_MODEL),
        "attn_q_gain": gain(ks[11], (N_ATTN_LAYERS, HEAD_DIM)),
        "attn_k_gain": gain(ks[12], (N_ATTN_LAYERS, HEAD_DIM)),
        "attn_w_out": nrm(ks[13], (N_ATTN_LAYERS, D_MODEL, D_MODEL), D_MODEL),
        "pool_w_group": nrm(ks[14], (N_POOL_LAYERS, N_POOL_GROUPS, POOL_GROUP_DIM, POOL_GROUP_DIM), POOL_GROUP_DIM),
        "pool_scale": gain(ks[15], (N_POOL_LAYERS, D_MODEL)),
        "sgu_w_in": nrm(ks[16], (N_SGU_LAYERS, D_MODEL, 2 * SGU_DIM), D_MODEL),
        "sgu_v_gain": gain(ks[17], (N_SGU_LAYERS, SGU_DIM)),
        "sgu_w_spatial": nrm(ks[18], (N_SGU_LAYERS, SGU_GROUPS, SGU_CHUNK, SGU_CHUNK), SGU_CHUNK, 0.5),
        "sgu_b_spatial": gain(ks[19], (N_SGU_LAYERS, SGU_GROUPS, SGU_CHUNK)),
        "sgu_w_out": nrm(ks[20], (N_SGU_LAYERS, SGU_DIM, D_MODEL), SGU_DIM),
    }


def reference(x, ffn1_norm, ffn1_w_gate, ffn1_w_up, ffn1_w_down, mix_norm, ffn2_norm,
              ffn2_w_gate, ffn2_w_up, ffn2_w_down, attn_w_qkv, attn_q_gain, attn_k_gain,
              attn_w_out, pool_w_group, pool_scale, sgu_w_in, sgu_v_gain, sgu_w_spatial,
              sgu_b_spatial, sgu_w_out):
    for i in range(DEPTH):
        x = x + 0.5 * _swiglu(_rms(x, ffn1_norm[i]), ffn1_w_gate[i], ffn1_w_up[i], ffn1_w_down[i])
        h = _rms(x, mix_norm[i])
        kind, j = i % N_MIXERS, i // N_MIXERS
        if kind == 0:
            y = _moba_attention(h, attn_w_qkv[j], attn_q_gain[j], attn_k_gain[j], attn_w_out[j])
        elif kind == 1:
            y = _pool_mixer(h, pool_w_group[j], pool_scale[j])
        else:
            y = _sgu_mixer(h, sgu_w_in[j], sgu_v_gain[j], sgu_w_spatial[j], sgu_b_spatial[j], sgu_w_out[j])
        x = x + y
        x = x + 0.5 * _swiglu(_rms(x, ffn2_norm[i]), ffn2_w_gate[i], ffn2_w_up[i], ffn2_w_down[i])
    return x
```

```python
import functools

import jax
import jax.numpy as jnp
import numpy as np
from jax import lax
from jax.experimental import pallas as pl
from jax.experimental.pallas import tpu as pltpu

D_MODEL = 2048
DEPTH = 4
N_MIXERS = 3
RMS_EPS = 1e-6
D_FF = 5504
N_HEADS = 16
HEAD_DIM = D_MODEL // N_HEADS
MOBA_BLOCK = 256
MOBA_TOPK = 3
POOL_WINDOWS = (2, 4, 8, 16)
N_POOL_GROUPS = len(POOL_WINDOWS)
POOL_GROUP_DIM = D_MODEL // N_POOL_GROUPS
POOL_HALO = max(POOL_WINDOWS)
SGU_DIM = 3 * D_MODEL
SGU_CHUNK = 128
SGU_GROUPS = 8
SGU_GROUP_DIM = SGU_DIM // SGU_GROUPS

LANES = 128
MXU_DIM = 256
VMEM_LIMIT = 52 * 1024 * 1024

F32 = jnp.float32
BF16 = jnp.bfloat16
MASK_BIAS = -1e9
NT_DIMS = (((1,), (1,)), ((), ()))


def _round_up(n, m):
    return -(-n // m) * m


D_FF_PAD = _round_up(D_FF, 2 * MXU_DIM)
FFN_TM = 512
FFN_TF = 512


def _params(*sem):
    return pltpu.CompilerParams(dimension_semantics=sem, vmem_limit_bytes=VMEM_LIMIT)


def _rms_rows(x, gain):
    ms = jnp.mean(x * x, axis=-1, keepdims=True)
    return x * lax.rsqrt(ms + RMS_EPS) * gain


def _ffn_kernel(x_ref, g_ref, wg_ref, wu_ref, wd_ref, o_ref, h_ref):
    @pl.when(pl.program_id(1) == 0)
    def _():
        x = x_ref[...]
        h_ref[...] = _rms_rows(x, g_ref[...]).astype(BF16)
        o_ref[...] = x

    h = h_ref[...]
    gate = jnp.dot(h, wg_ref[...], preferred_element_type=F32)
    up = jnp.dot(h, wu_ref[...], preferred_element_type=F32)
    a = (gate * jax.nn.sigmoid(gate)) * up * 0.5
    o_ref[...] += jnp.dot(a.astype(BF16), wd_ref[...], preferred_element_type=F32)


def _ffn(x, gain, wg, wu, wd):
    t, d = x.shape
    tm, tf = FFN_TM, FFN_TF
    return pl.pallas_call(
        _ffn_kernel,
        grid=(t // tm, D_FF_PAD // tf),
        in_specs=[
            pl.BlockSpec((tm, d), lambda i, f: (i, 0)),
            pl.BlockSpec((1, d), lambda i, f: (0, 0)),
            pl.BlockSpec((d, tf), lambda i, f: (0, f)),
            pl.BlockSpec((d, tf), lambda i, f: (0, f)),
            pl.BlockSpec((tf, d), lambda i, f: (f, 0)),
        ],
        out_specs=pl.BlockSpec((tm, d), lambda i, f: (i, 0)),
        out_shape=jax.ShapeDtypeStruct((t, d), F32),
        scratch_shapes=[pltpu.VMEM((tm, d), BF16)],
        compiler_params=_params("parallel", "arbitrary"),
        name="ffn",
    )(x, gain.reshape(1, d), wg, wu, wd)


def _prep_ffn_weights(w_gate, w_up, w_down):
    pad = D_FF_PAD - D_FF
    wg = jnp.pad(w_gate.astype(BF16), ((0, 0), (0, pad)))
    wu = jnp.pad(w_up.astype(BF16), ((0, 0), (0, pad)))
    wd = jnp.pad(w_down.astype(BF16), ((0, pad), (0, 0)))
    return wg, wu, wd


def _qkv_kernel(x_ref, g_ref, w_ref, qg_ref, kg_ref, q_ref, k_ref, v_ref, h_ref, *, n_q):
    j = pl.program_id(1)

    @pl.when(j == 0)
    def _():
        h_ref[...] = _rms_rows(x_ref[...], g_ref[...]).astype(BF16)

    y = jnp.dot(h_ref[...], w_ref[...], preferred_element_type=F32)
    heads = y.shape[1] // HEAD_DIM

    def head_norm(gain):
        cols = []
        for hh in range(heads):
            yh = y[:, hh * HEAD_DIM:(hh + 1) * HEAD_DIM]
            cols.append(_rms_rows(yh, gain))
        return jnp.concatenate(cols, axis=1)

    @pl.when(j < n_q)
    def _():
        q_ref[...] = head_norm(qg_ref[...])

    @pl.when(jnp.logical_and(j >= n_q, j < 2 * n_q))
    def _():
        k_ref[...] = head_norm(kg_ref[...]).astype(BF16)

    @pl.when(j >= 2 * n_q)
    def _():
        v_ref[...] = y.astype(BF16)


def _qkv_proj(x, gain, w, q_gain, k_gain):
    t, d = x.shape
    tm, tn = 512, 512
    n_q = d // tn
    kern = functools.partial(_qkv_kernel, n_q=n_q)
    return pl.pallas_call(
        kern,
        grid=(t // tm, 3 * n_q),
        in_specs=[
            pl.BlockSpec((tm, d), lambda i, j: (i, 0)),
            pl.BlockSpec((1, d), lambda i, j: (0, 0)),
            pl.BlockSpec((d, tn), lambda i, j: (0, j)),
            pl.BlockSpec((1, HEAD_DIM), lambda i, j: (0, 0)),
            pl.BlockSpec((1, HEAD_DIM), lambda i, j: (0, 0)),
        ],
        out_specs=[
            pl.BlockSpec((tm, tn), lambda i, j: (i, jnp.minimum(j, n_q - 1))),
            pl.BlockSpec((tm, tn), lambda i, j: (i, jnp.clip(j - n_q, 0, n_q - 1))),
            pl.BlockSpec((tm, tn), lambda i, j: (i, jnp.maximum(j - 2 * n_q, 0))),
        ],
        out_shape=[
            jax.ShapeDtypeStruct((t, d), F32),
            jax.ShapeDtypeStruct((t, d), BF16),
            jax.ShapeDtypeStruct((t, d), BF16),
        ],
        scratch_shapes=[pltpu.VMEM((tm, d), BF16)],
        compiler_params=_params("parallel", "arbitrary"),
        name="qkv_proj",
    )(x, gain.reshape(1, d), w, q_gain.reshape(1, HEAD_DIM), k_gain.reshape(1, HEAD_DIM))


def _attn_kernel(slope_ref, q_ref, k_ref, v_ref, o_ref, kmean_ref, m_ref, l_ref, acc_ref, *, n_blocks):
    bs = MOBA_BLOCK
    i = pl.program_id(2)
    scale = HEAD_DIM ** -0.5

    @pl.when(i == 0)
    def _():
        for j in range(n_blocks):
            kj = k_ref[j * bs:(j + 1) * bs, :].astype(F32)
            kmean_ref[j:j + 1, :] = jnp.mean(kj, axis=0, keepdims=True)

    q32 = q_ref[...]
    qb = q32.astype(BF16)

    gate = lax.dot_general(kmean_ref[...], q32, NT_DIMS, precision=lax.Precision.HIGHEST,
                           preferred_element_type=F32)
    blk = lax.broadcasted_iota(jnp.int32, gate.shape, 0)
    rank = jnp.zeros(gate.shape, jnp.int32)
    for jp in range(n_blocks):
        gj = gate[jp:jp + 1, :]
        beats = jnp.logical_or(gj > gate, jnp.logical_and(gj == gate, blk > jp))
        rank = rank + jnp.where(jnp.logical_and(beats, jp < i), 1, 0)
    selected = jnp.logical_and(blk < i, rank < MOBA_TOPK)
    bias_t = jnp.where(selected, 0.0, MASK_BIAS).astype(F32)
    bias_t = jnp.concatenate([bias_t, jnp.zeros((LANES - n_blocks, bs), F32)], axis=0)
    bias = bias_t.T.astype(BF16)
    q_aug = jnp.concatenate([qb, bias], axis=1)

    slope = slope_ref[0]
    row = lax.broadcasted_iota(jnp.int32, (bs, bs), 0)
    col = lax.broadcasted_iota(jnp.int32, (bs, bs), 1)
    rel = (row - col).astype(F32)
    alibi0 = slope * rel

    k_own = k_ref[pl.ds(pl.multiple_of(i * bs, bs), bs), :]
    v_own = v_ref[pl.ds(pl.multiple_of(i * bs, bs), bs), :]
    s = lax.dot_general(qb, k_own, NT_DIMS, preferred_element_type=F32)
    t = jnp.where(rel >= 0, s * scale - alibi0, -jnp.inf)
    m0 = jnp.max(t, axis=1, keepdims=True)
    p = jnp.exp(t - m0)
    m_ref[...] = jnp.broadcast_to(m0, m_ref.shape)
    l_ref[...] = jnp.broadcast_to(jnp.sum(p, axis=1, keepdims=True), l_ref.shape)
    acc_ref[...] = jnp.dot(p.astype(BF16), v_own, preferred_element_type=F32)

    lane = lax.broadcasted_iota(jnp.int32, (bs, LANES), 1)

    def past_block(j, carry):
        start = pl.multiple_of(j * bs, bs)
        kj = k_ref[pl.ds(start, bs), :]
        vj = v_ref[pl.ds(start, bs), :]
        onehot = jnp.where(lane == j, 1.0, 0.0).astype(BF16)
        k_aug = jnp.concatenate([kj, onehot], axis=1)
        s = lax.dot_general(q_aug, k_aug, NT_DIMS, preferred_element_type=F32)
        block_dist = ((i - j) * bs).astype(F32)
        t = s * scale - alibi0 - slope * block_dist
        m_prev = m_ref[...][:, :1]
        l_prev = l_ref[...][:, :1]
        m_new = jnp.maximum(m_prev, jnp.max(t, axis=1, keepdims=True))
        alpha = jnp.exp(m_prev - m_new)
        p = jnp.exp(t - m_new)
        l_new = alpha * l_prev + jnp.sum(p, axis=1, keepdims=True)
        acc_ref[...] = alpha * acc_ref[...] + jnp.dot(p.astype(BF16), vj, preferred_element_type=F32)
        m_ref[...] = jnp.broadcast_to(m_new, m_ref.shape)
        l_ref[...] = jnp.broadcast_to(l_new, l_ref.shape)
        return carry

    lax.fori_loop(0, i, past_block, 0)
    o_ref[...] = (acc_ref[...] / l_ref[...]).astype(o_ref.dtype)


def _moba_attention(q, k, v, batch, seq):
    t, d = q.shape
    bs = MOBA_BLOCK
    n_blocks = seq // bs
    slopes = np.asarray(2.0 ** (-8.0 * np.arange(1, N_HEADS + 1) / N_HEADS), dtype=np.float32)
    slopes = jnp.asarray(np.broadcast_to(slopes[:, None, None], (N_HEADS, 1, bs)))
    kern = functools.partial(_attn_kernel, n_blocks=n_blocks)
    return pl.pallas_call(
        kern,
        grid=(batch, N_HEADS, n_blocks),
        in_specs=[
            pl.BlockSpec((1, 1, bs), lambda b, h, i: (h, 0, 0)),
            pl.BlockSpec((bs, HEAD_DIM), lambda b, h, i: (b * n_blocks + i, h)),
            pl.BlockSpec((seq, HEAD_DIM), lambda b, h, i: (b, h)),
            pl.BlockSpec((seq, HEAD_DIM), lambda b, h, i: (b, h)),
        ],
        out_specs=pl.BlockSpec((bs, HEAD_DIM), lambda b, h, i: (b * n_blocks + i, h)),
        out_shape=jax.ShapeDtypeStruct((t, d), BF16),
        scratch_shapes=[
            pltpu.VMEM((n_blocks, HEAD_DIM), F32),
            pltpu.VMEM((bs, LANES), F32),
            pltpu.VMEM((bs, LANES), F32),
            pltpu.VMEM((bs, HEAD_DIM), F32),
        ],
        compiler_params=_params("parallel", "parallel", "arbitrary"),
        name="moba_attn",
    )(slopes, q, k, v)


def _proj_res_kernel(x_ref, a_ref, w_ref, o_ref):
    o_ref[...] = x_ref[...] + jnp.dot(a_ref[...], w_ref[...], preferred_element_type=F32)


def _proj_residual(x, a, w):
    t, n = x.shape
    kdim = a.shape[1]
    tm, tn = 512, 1024
    return pl.pallas_call(
        _proj_res_kernel,
        grid=(t // tm, n // tn),
        in_specs=[
            pl.BlockSpec((tm, tn), lambda i, j: (i, j)),
            pl.BlockSpec((tm, kdim), lambda i, j: (i, 0)),
            pl.BlockSpec((kdim, tn), lambda i, j: (0, j)),
        ],
        out_specs=pl.BlockSpec((tm, tn), lambda i, j: (i, j)),
        out_shape=jax.ShapeDtypeStruct((t, n), F32),
        compiler_params=_params("parallel", "parallel"),
        name="proj_residual",
    )(x, a, w)


def _pool_kernel(x_ref, halo_ref, g_ref, w_ref, sc_ref, o_ref, *, ts):
    i = pl.program_id(1)
    x = x_ref[...]
    gain = g_ref[...]
    h = _rms_rows(x, gain)
    h_halo = jnp.where(i == 0, 0.0, _rms_rows(halo_ref[...], gain))
    he = jnp.concatenate([h_halo, h], axis=0)
    pos = i * ts + lax.broadcasted_iota(jnp.int32, (ts, 1), 0)
    count = (pos + 1).astype(F32)
    cg = POOL_GROUP_DIM
    outs = []
    for g, win in enumerate(POOL_WINDOWS):
        a = he[:, g * cg:(g + 1) * cg]
        lead = 0
        width = 1
        while width < win:
            a = a[width:, :] + a[:-width, :]
            lead += width
            width *= 2
        start = POOL_HALO - lead
        pooled = a[start:start + ts, :] / jnp.minimum(count, float(win)) - h[:, g * cg:(g + 1) * cg]
        outs.append(jnp.dot(pooled.astype(BF16), w_ref[g], preferred_element_type=F32))
    y = jnp.concatenate(outs, axis=1)
    o_ref[...] = x + y * sc_ref[...]


def _pool_mixer(x, gain, w_group, scale, batch, seq):
    t, d = x.shape
    ts = 512
    per_seq = seq // ts
    halo_per_tile = ts // POOL_HALO
    kern = functools.partial(_pool_kernel, ts=ts)

    def halo_map(b, i):
        return (jnp.maximum((b * per_seq + i) * halo_per_tile - 1, 0), 0)

    return pl.pallas_call(
        kern,
        grid=(batch, per_seq),
        in_specs=[
            pl.BlockSpec((ts, d), lambda b, i: (b * per_seq + i, 0)),
            pl.BlockSpec((POOL_HALO, d), halo_map),
            pl.BlockSpec((1, d), lambda b, i: (0, 0)),
            pl.BlockSpec((N_POOL_GROUPS, POOL_GROUP_DIM, POOL_GROUP_DIM), lambda b, i: (0, 0, 0)),
            pl.BlockSpec((1, d), lambda b, i: (0, 0)),
        ],
        out_specs=pl.BlockSpec((ts, d), lambda b, i: (b * per_seq + i, 0)),
        out_shape=jax.ShapeDtypeStruct((t, d), F32),
        compiler_params=_params("parallel", "parallel"),
        name="pool_mixer",
    )(x, x, gain.reshape(1, d), w_group.astype(BF16), scale.reshape(1, d))


def _sgu_in_kernel(x_ref, g_ref, w_ref, z_ref, ssq_ref, h_ref, *, n_u):
    j = pl.program_id(1)

    @pl.when(j == 0)
    def _():
        h_ref[...] = _rms_rows(x_ref[...], g_ref[...]).astype(BF16)
        ssq_ref[...] = jnp.zeros_like(ssq_ref)

    y = jnp.dot(h_ref[...], w_ref[...], preferred_element_type=F32)
    z = 0.5 * y * (1.0 + lax.erf(y * np.float32(np.sqrt(0.5))))
    z_ref[...] = z.astype(BF16)

    @pl.when(j >= n_u)
    def _():
        zz = z * z
        part = zz[:, :LANES]
        for c in range(1, zz.shape[1] // LANES):
            part = part + zz[:, c * LANES:(c + 1) * LANES]
        ssq_ref[...] += part


def _sgu_in(x, gain, w_in):
    t, d = x.shape
    n = w_in.shape[1]
    tm, tn = 512, 512
    n_u = (n // 2) // tn
    kern = functools.partial(_sgu_in_kernel, n_u=n_u)
    return pl.pallas_call(
        kern,
        grid=(t // tm, n // tn),
        in_specs=[
            pl.BlockSpec((tm, d), lambda i, j: (i, 0)),
            pl.BlockSpec((1, d), lambda i, j: (0, 0)),
            pl.BlockSpec((d, tn), lambda i, j: (0, j)),
        ],
        out_specs=[
            pl.BlockSpec((tm, tn), lambda i, j: (i, j)),
            pl.BlockSpec((tm, LANES), lambda i, j: (i, 0)),
        ],
        out_shape=[
            jax.ShapeDtypeStruct((t, n), BF16),
            jax.ShapeDtypeStruct((t, LANES), F32),
        ],
        scratch_shapes=[pltpu.VMEM((tm, d), BF16)],
        compiler_params=_params("parallel", "arbitrary"),
        name="sgu_in",
    )(x, gain.reshape(1, d), w_in)


def _sgu_out_kernel(x_ref, u_ref, v_ref, ssq_ref, vg_ref, ws_ref, bs_ref, wo_ref, o_ref, *, tm):
    g = pl.program_id(1)

    @pl.when(g == 0)
    def _():
        o_ref[...] = x_ref[...]

    inv = lax.rsqrt(jnp.sum(ssq_ref[...], axis=1, keepdims=True) / SGU_DIM + RMS_EPS)
    vn = (v_ref[...].astype(F32) * inv * vg_ref[...]).astype(BF16)
    row = lax.broadcasted_iota(jnp.int32, (SGU_CHUNK, SGU_CHUNK), 0)
    col = lax.broadcasted_iota(jnp.int32, (SGU_CHUNK, SGU_CHUNK), 1)
    w_s = jnp.where(row >= col, ws_ref[0], 0.0).astype(BF16)
    bias = bs_ref[0][:, :1]
    parts = []
    for c in range(tm // SGU_CHUNK):
        vc = vn[c * SGU_CHUNK:(c + 1) * SGU_CHUNK, :]
        parts.append(jnp.dot(w_s, vc, preferred_element_type=F32) + bias)
    sv = jnp.concatenate(parts, axis=0)
    p = (u_ref[...].astype(F32) * sv).astype(BF16)
    o_ref[...] += jnp.dot(p, wo_ref[...], preferred_element_type=F32)


def _sgu_out(x, z, ssq, v_gain, w_spatial, b_spatial, w_out):
    t, d = x.shape
    tm = 512
    gd = SGU_GROUP_DIM
    b_rep = jnp.broadcast_to(b_spatial[:, :, None], (SGU_GROUPS, SGU_CHUNK, LANES))
    kern = functools.partial(_sgu_out_kernel, tm=tm)
    return pl.pallas_call(
        kern,
        grid=(t // tm, SGU_GROUPS),
        in_specs=[
            pl.BlockSpec((tm, d), lambda i, g: (i, 0)),
            pl.BlockSpec((tm, gd), lambda i, g: (i, g)),
            pl.BlockSpec((tm, gd), lambda i, g: (i, SGU_GROUPS + g)),
            pl.BlockSpec((tm, LANES), lambda i, g: (i, 0)),
            pl.BlockSpec((1, gd), lambda i, g: (0, g)),
            pl.BlockSpec((1, SGU_CHUNK, SGU_CHUNK), lambda i, g: (g, 0, 0)),
            pl.BlockSpec((1, SGU_CHUNK, LANES), lambda i, g: (g, 0, 0)),
            pl.BlockSpec((gd, d), lambda i, g: (g, 0)),
        ],
        out_specs=pl.BlockSpec((tm, d), lambda i, g: (i, 0)),
        out_shape=jax.ShapeDtypeStruct((t, d), F32),
        compiler_params=_params("parallel", "arbitrary"),
        name="sgu_out",
    )(x, z, z, ssq, v_gain.reshape(1, SGU_DIM), w_spatial, b_rep, w_out)


def kernel(x, ffn1_norm, ffn1_w_gate, ffn1_w_up, ffn1_w_down, mix_norm, ffn2_norm, ffn2_w_gate, ffn2_w_up, ffn2_w_down, attn_w_qkv, attn_q_gain, attn_k_gain, attn_w_out, pool_w_group, pool_scale, sgu_w_in, sgu_v_gain, sgu_w_spatial, sgu_b_spatial, sgu_w_out):
    batch, seq, d = x.shape
    assert d == D_MODEL and seq % MOBA_BLOCK == 0 and seq % SGU_CHUNK == 0
    xt = x.reshape(batch * seq, d)
    for i in range(DEPTH):
        xt = _ffn(xt, ffn1_norm[i], *_prep_ffn_weights(ffn1_w_gate[i], ffn1_w_up[i], ffn1_w_down[i]))
        kind, j = i % N_MIXERS, i // N_MIXERS
        if kind == 0:
            q, k, v = _qkv_proj(xt, mix_norm[i], attn_w_qkv[j].astype(BF16), attn_q_gain[j], attn_k_gain[j])
            o = _moba_attention(q, k, v, batch, seq)
            xt = _proj_residual(xt, o, attn_w_out[j].astype(BF16))
        elif kind == 1:
            xt = _pool_mixer(xt, mix_norm[i], pool_w_group[j], pool_scale[j], batch, seq)
        else:
            z, ssq = _sgu_in(xt, mix_norm[i], sgu_w_in[j].astype(BF16))
            xt = _sgu_out(xt, z, ssq, sgu_v_gain[j], sgu_w_spatial[j], sgu_b_spatial[j],
                          sgu_w_out[j].astype(BF16))
        xt = _ffn(xt, ffn2_norm[i], *_prep_ffn_weights(ffn2_w_gate[i], ffn2_w_up[i], ffn2_w_down[i]))
    return xt.reshape(batch, seq, d)
```

```python
import functools

import jax
import jax.numpy as jnp
import numpy as np
from jax import lax
from jax.experimental import pallas as pl
from jax.experimental.pallas import tpu as pltpu

D_MODEL = 2048
DEPTH = 4
N_MIXERS = 3
RMS_EPS = 1e-6
D_FF = 5504
N_HEADS = 16
HEAD_DIM = D_MODEL // N_HEADS
MOBA_BLOCK = 256
MOBA_TOPK = 3
POOL_WINDOWS = (2, 4, 8, 16)
N_POOL_GROUPS = len(POOL_WINDOWS)
POOL_GROUP_DIM = D_MODEL // N_POOL_GROUPS
POOL_HALO = max(POOL_WINDOWS)
SGU_DIM = 3 * D_MODEL
SGU_CHUNK = 128
SGU_GROUPS = 8
SGU_GROUP_DIM = SGU_DIM // SGU_GROUPS

LANES = 128
MXU_DIM = 256
VMEM_LIMIT = 52 * 1024 * 1024

F32 = jnp.float32
BF16 = jnp.bfloat16
MASK_BIAS = -1e9
LOG2E = float(np.log2(np.e))
ATTN_GROUP = 4
NT_DIMS = (((1,), (1,)), ((), ()))


def _round_up(n, m):
    return -(-n // m) * m


D_FF_PAD = _round_up(D_FF, 2 * MXU_DIM)
FFN_TM = 1024
FFN_TF = 512


def _params(*sem):
    return pltpu.CompilerParams(dimension_semantics=sem, vmem_limit_bytes=VMEM_LIMIT)


def _rms_rows(x, gain):
    ms = jnp.mean(x * x, axis=-1, keepdims=True)
    return x * lax.rsqrt(ms + RMS_EPS) * gain


def _ffn_kernel(x_ref, g_ref, wg_ref, wu_ref, wd_ref, o_ref, h_ref):
    @pl.when(pl.program_id(1) == 0)
    def _():
        x = x_ref[...]
        h_ref[...] = _rms_rows(x, g_ref[...]).astype(BF16)
        o_ref[...] = x

    h = h_ref[...]
    gate = jnp.dot(h, wg_ref[...], preferred_element_type=F32)
    up = jnp.dot(h, wu_ref[...], preferred_element_type=F32)
    a = (gate * jax.nn.sigmoid(gate)) * up * 0.5
    o_ref[...] += jnp.dot(a.astype(BF16), wd_ref[...], preferred_element_type=F32)


def _ffn(x, gain, wg, wu, wd):
    t, d = x.shape
    tm, tf = FFN_TM, FFN_TF
    return pl.pallas_call(
        _ffn_kernel,
        grid=(t // tm, D_FF_PAD // tf),
        in_specs=[
            pl.BlockSpec((tm, d), lambda i, f: (i, 0), pipeline_mode=pl.Buffered(1)),
            pl.BlockSpec((1, d), lambda i, f: (0, 0)),
            pl.BlockSpec((d, tf), lambda i, f: (0, f)),
            pl.BlockSpec((d, tf), lambda i, f: (0, f)),
            pl.BlockSpec((tf, d), lambda i, f: (f, 0)),
        ],
        out_specs=pl.BlockSpec((tm, d), lambda i, f: (i, 0)),
        out_shape=jax.ShapeDtypeStruct((t, d), F32),
        scratch_shapes=[pltpu.VMEM((tm, d), BF16)],
        compiler_params=_params("parallel", "arbitrary"),
        name="ffn",
    )(x, gain.reshape(1, d), wg, wu, wd)


def _prep_ffn_weights(w_gate, w_up, w_down):
    pad = D_FF_PAD - D_FF
    wg = jnp.pad(w_gate.astype(BF16), ((0, 0), (0, pad)))
    wu = jnp.pad(w_up.astype(BF16), ((0, 0), (0, pad)))
    wd = jnp.pad(w_down.astype(BF16), ((0, pad), (0, 0)))
    return wg, wu, wd


def _qkv_kernel(x_ref, g_ref, w_ref, qg_ref, kg_ref, q_ref, k_ref, v_ref, h_ref, *, n_q):
    j = pl.program_id(1)

    @pl.when(j == 0)
    def _():
        h_ref[...] = _rms_rows(x_ref[...], g_ref[...]).astype(BF16)

    y = jnp.dot(h_ref[...], w_ref[...], preferred_element_type=F32)
    heads = y.shape[1] // HEAD_DIM

    def head_norm(gain):
        cols = []
        for hh in range(heads):
            yh = y[:, hh * HEAD_DIM:(hh + 1) * HEAD_DIM]
            cols.append(_rms_rows(yh, gain))
        return jnp.concatenate(cols, axis=1)

    @pl.when(j < n_q)
    def _():
        q_ref[...] = head_norm(qg_ref[...])

    @pl.when(jnp.logical_and(j >= n_q, j < 2 * n_q))
    def _():
        k_ref[...] = head_norm(kg_ref[...]).astype(BF16)

    @pl.when(j >= 2 * n_q)
    def _():
        v_ref[...] = y.astype(BF16)


def _qkv_proj(x, gain, w, q_gain, k_gain):
    t, d = x.shape
    tm, tn = 512, 512
    n_q = d // tn
    kern = functools.partial(_qkv_kernel, n_q=n_q)
    return pl.pallas_call(
        kern,
        grid=(t // tm, 3 * n_q),
        in_specs=[
            pl.BlockSpec((tm, d), lambda i, j: (i, 0)),
            pl.BlockSpec((1, d), lambda i, j: (0, 0)),
            pl.BlockSpec((d, tn), lambda i, j: (0, j)),
            pl.BlockSpec((1, HEAD_DIM), lambda i, j: (0, 0)),
            pl.BlockSpec((1, HEAD_DIM), lambda i, j: (0, 0)),
        ],
        out_specs=[
            pl.BlockSpec((tm, tn), lambda i, j: (i, jnp.minimum(j, n_q - 1))),
            pl.BlockSpec((tm, tn), lambda i, j: (i, jnp.clip(j - n_q, 0, n_q - 1))),
            pl.BlockSpec((tm, tn), lambda i, j: (i, jnp.maximum(j - 2 * n_q, 0))),
        ],
        out_shape=[
            jax.ShapeDtypeStruct((t, d), F32),
            jax.ShapeDtypeStruct((t, d), BF16),
            jax.ShapeDtypeStruct((t, d), BF16),
        ],
        scratch_shapes=[pltpu.VMEM((tm, d), BF16)],
        compiler_params=_params("parallel", "arbitrary"),
        name="qkv_proj",
    )(x, gain.reshape(1, d), w, q_gain.reshape(1, HEAD_DIM), k_gain.reshape(1, HEAD_DIM))


def _attn_kernel(slope_ref, q_ref, k_ref, v_ref, o_ref, kt_ref, kmean_ref, m_ref, l_ref, acc_ref, *,
                 n_blocks, group):
    bs = MOBA_BLOCK
    i = pl.program_id(2)
    c1 = HEAD_DIM ** -0.5 * LOG2E

    @pl.when(i == 0)
    def _():
        feat = lax.broadcasted_iota(jnp.int32, (LANES, bs), 0)
        for j in range(n_blocks):
            kj = k_ref[j * bs:(j + 1) * bs, :].astype(F32)
            kmean_ref[j:j + 1, :] = jnp.mean(kj, axis=0, keepdims=True)
            kt_ref[j, :HEAD_DIM, :] = kj.T.astype(BF16)
            kt_ref[j, HEAD_DIM:, :] = jnp.where(feat == j, 1.0, 0.0).astype(BF16)

    q32 = q_ref[...]
    qb = q32.astype(BF16)

    gate = lax.dot_general(kmean_ref[...], q32, NT_DIMS, precision=lax.Precision.HIGHEST,
                           preferred_element_type=F32)
    blk = lax.broadcasted_iota(jnp.int32, gate.shape, 0)
    rank = jnp.zeros(gate.shape, jnp.int32)
    for jp in range(n_blocks - 1):
        gj = gate[jp:jp + 1, :]
        beats = jnp.logical_or(gj > gate, jnp.logical_and(gj == gate, blk > jp))
        rank = rank + jnp.where(jnp.logical_and(beats, jp < i), 1, 0)
    selected = jnp.logical_and(blk < i, rank < MOBA_TOPK)
    bias_t = jnp.where(selected, 0.0, MASK_BIAS).astype(F32)
    bias_t = jnp.concatenate([bias_t, jnp.zeros((LANES - n_blocks, bs), F32)], axis=0)
    q_aug = jnp.concatenate([qb, bias_t.T.astype(BF16)], axis=1)

    slope2 = slope_ref[0] * LOG2E
    key_off = lax.broadcasted_iota(jnp.int32, (1, bs), 1)

    def col_bias(j):
        return slope2 * ((j - i) * bs + key_off).astype(F32)

    def fold(x):
        return x[:, :LANES] + x[:, LANES:]

    row = lax.broadcasted_iota(jnp.int32, (bs, bs), 0)
    col = lax.broadcasted_iota(jnp.int32, (bs, bs), 1)
    s = jnp.dot(qb, kt_ref[i, :HEAD_DIM, :], preferred_element_type=F32)
    t = jnp.where(row >= col, s * c1 + col_bias(i), -jnp.inf)
    m0 = jnp.max(t, axis=1, keepdims=True)
    p = jnp.exp2(t - m0)
    v_own = v_ref[pl.ds(pl.multiple_of(i * bs, bs), bs), :]
    m_ref[...] = jnp.broadcast_to(m0, m_ref.shape)
    l_ref[...] = fold(p)
    acc_ref[...] = jnp.dot(p.astype(BF16), v_own, preferred_element_type=F32)

    for g0 in range(0, n_blocks - 1, group):
        blocks = range(g0, min(g0 + group, n_blocks - 1))

        @pl.when(g0 < i)
        def _(blocks=blocks):
            ts = []
            for j in blocks:
                s = jnp.dot(q_aug, kt_ref[j], preferred_element_type=F32)
                ts.append(s * c1 + col_bias(j))
            tmax = functools.reduce(jnp.maximum, ts)
            m_blk = jnp.max(jnp.maximum(tmax[:, :LANES], tmax[:, LANES:]), axis=1, keepdims=True)
            m_prev = m_ref[...]
            m_new = jnp.maximum(m_prev, m_blk)
            alpha = jnp.exp2(m_prev - m_new)
            m2 = jnp.concatenate([m_new, m_new], axis=1)
            ps = [jnp.exp2(t - m2) for t in ts]
            psum = functools.reduce(lambda a, b: a + b, [fold(p) for p in ps])
            pcat = jnp.concatenate([p.astype(BF16) for p in ps], axis=1)
            vg = v_ref[blocks[0] * bs:(blocks[-1] + 1) * bs, :]
            pv = jnp.dot(pcat, vg, preferred_element_type=F32)
            l_ref[...] = alpha * l_ref[...] + psum
            acc_ref[...] = alpha * acc_ref[...] + pv
            m_ref[...] = m_new

    l_tot = jnp.sum(l_ref[...], axis=1, keepdims=True)
    o_ref[...] = (acc_ref[...] / l_tot).astype(o_ref.dtype)


def _moba_attention(q, k, v, batch, seq):
    t, d = q.shape
    bs = MOBA_BLOCK
    n_blocks = seq // bs
    slopes = np.asarray(2.0 ** (-8.0 * np.arange(1, N_HEADS + 1) / N_HEADS), dtype=np.float32)
    slopes = jnp.asarray(np.broadcast_to(slopes[:, None, None], (N_HEADS, 1, bs)))
    kern = functools.partial(_attn_kernel, n_blocks=n_blocks, group=ATTN_GROUP)
    return pl.pallas_call(
        kern,
        grid=(batch, N_HEADS, n_blocks),
        in_specs=[
            pl.BlockSpec((1, 1, bs), lambda b, h, i: (h, 0, 0)),
            pl.BlockSpec((bs, HEAD_DIM), lambda b, h, i: (b * n_blocks + i, h)),
            pl.BlockSpec((seq, HEAD_DIM), lambda b, h, i: (b, h)),
            pl.BlockSpec((seq, HEAD_DIM), lambda b, h, i: (b, h)),
        ],
        out_specs=pl.BlockSpec((bs, HEAD_DIM), lambda b, h, i: (b * n_blocks + i, h)),
        out_shape=jax.ShapeDtypeStruct((t, d), BF16),
        scratch_shapes=[
            pltpu.VMEM((n_blocks, HEAD_DIM + LANES, bs), BF16),
            pltpu.VMEM((n_blocks, HEAD_DIM), F32),
            pltpu.VMEM((bs, LANES), F32),
            pltpu.VMEM((bs, LANES), F32),
            pltpu.VMEM((bs, HEAD_DIM), F32),
        ],
        compiler_params=_params("parallel", "parallel", "arbitrary"),
        name="moba_attn",
    )(slopes, q, k, v)


def _proj_res_kernel(x_ref, a_ref, w_ref, o_ref):
    o_ref[...] = x_ref[...] + jnp.dot(a_ref[...], w_ref[...], preferred_element_type=F32)


def _proj_residual(x, a, w):
    t, n = x.shape
    kdim = a.shape[1]
    tm, tn = 512, 1024
    return pl.pallas_call(
        _proj_res_kernel,
        grid=(t // tm, n // tn),
        in_specs=[
            pl.BlockSpec((tm, tn), lambda i, j: (i, j)),
            pl.BlockSpec((tm, kdim), lambda i, j: (i, 0)),
            pl.BlockSpec((kdim, tn), lambda i, j: (0, j)),
        ],
        out_specs=pl.BlockSpec((tm, tn), lambda i, j: (i, j)),
        out_shape=jax.ShapeDtypeStruct((t, n), F32),
        compiler_params=_params("parallel", "parallel"),
        name="proj_residual",
    )(x, a, w)


def _pool_kernel(x_ref, halo_ref, g_ref, w_ref, sc_ref, o_ref, *, ts):
    i = pl.program_id(1)
    x = x_ref[...]
    gain = g_ref[...]
    h = _rms_rows(x, gain)
    h_halo = jnp.where(i == 0, 0.0, _rms_rows(halo_ref[...], gain))
    he = jnp.concatenate([h_halo, h], axis=0)
    pos = i * ts + lax.broadcasted_iota(jnp.int32, (ts, 1), 0)
    count = (pos + 1).astype(F32)
    cg = POOL_GROUP_DIM
    outs = []
    for g, win in enumerate(POOL_WINDOWS):
        a = he[:, g * cg:(g + 1) * cg]
        lead = 0
        width = 1
        while width < win:
            a = a[width:, :] + a[:-width, :]
            lead += width
            width *= 2
        start = POOL_HALO - lead
        pooled = a[start:start + ts, :] / jnp.minimum(count, float(win)) - h[:, g * cg:(g + 1) * cg]
        outs.append(jnp.dot(pooled.astype(BF16), w_ref[g], preferred_element_type=F32))
    y = jnp.concatenate(outs, axis=1)
    o_ref[...] = x + y * sc_ref[...]


def _pool_mixer(x, gain, w_group, scale, batch, seq):
    t, d = x.shape
    ts = 512
    per_seq = seq // ts
    halo_per_tile = ts // POOL_HALO
    kern = functools.partial(_pool_kernel, ts=ts)

    def halo_map(b, i):
        return (jnp.maximum((b * per_seq + i) * halo_per_tile - 1, 0), 0)

    return pl.pallas_call(
        kern,
        grid=(batch, per_seq),
        in_specs=[
            pl.BlockSpec((ts, d), lambda b, i: (b * per_seq + i, 0)),
            pl.BlockSpec((POOL_HALO, d), halo_map),
            pl.BlockSpec((1, d), lambda b, i: (0, 0)),
            pl.BlockSpec((N_POOL_GROUPS, POOL_GROUP_DIM, POOL_GROUP_DIM), lambda b, i: (0, 0, 0)),
            pl.BlockSpec((1, d), lambda b, i: (0, 0)),
        ],
        out_specs=pl.BlockSpec((ts, d), lambda b, i: (b * per_seq + i, 0)),
        out_shape=jax.ShapeDtypeStruct((t, d), F32),
        compiler_params=_params("parallel", "parallel"),
        name="pool_mixer",
    )(x, x, gain.reshape(1, d), w_group.astype(BF16), scale.reshape(1, d))


def _sgu_in_kernel(x_ref, g_ref, w_ref, z_ref, ssq_ref, h_ref, *, n_u):
    j = pl.program_id(1)

    @pl.when(j == 0)
    def _():
        h_ref[...] = _rms_rows(x_ref[...], g_ref[...]).astype(BF16)
        ssq_ref[...] = jnp.zeros_like(ssq_ref)

    y = jnp.dot(h_ref[...], w_ref[...], preferred_element_type=F32)
    z = 0.5 * y * (1.0 + lax.erf(y * np.float32(np.sqrt(0.5))))
    z_ref[...] = z.astype(BF16)

    @pl.when(j >= n_u)
    def _():
        zz = z * z
        part = zz[:, :LANES]
        for c in range(1, zz.shape[1] // LANES):
            part = part + zz[:, c * LANES:(c + 1) * LANES]
        ssq_ref[...] += part


def _sgu_in(x, gain, w_in):
    t, d = x.shape
    n = w_in.shape[1]
    tm, tn = 512, 512
    n_u = (n // 2) // tn
    kern = functools.partial(_sgu_in_kernel, n_u=n_u)
    return pl.pallas_call(
        kern,
        grid=(t // tm, n // tn),
        in_specs=[
            pl.BlockSpec((tm, d), lambda i, j: (i, 0)),
            pl.BlockSpec((1, d), lambda i, j: (0, 0)),
            pl.BlockSpec((d, tn), lambda i, j: (0, j)),
        ],
        out_specs=[
            pl.BlockSpec((tm, tn), lambda i, j: (i, j)),
            pl.BlockSpec((tm, LANES), lambda i, j: (i, 0)),
        ],
        out_shape=[
            jax.ShapeDtypeStruct((t, n), BF16),
            jax.ShapeDtypeStruct((t, LANES), F32),
        ],
        scratch_shapes=[pltpu.VMEM((tm, d), BF16)],
        compiler_params=_params("parallel", "arbitrary"),
        name="sgu_in",
    )(x, gain.reshape(1, d), w_in)


def _sgu_out_kernel(x_ref, u_ref, v_ref, ssq_ref, vg_ref, ws_ref, bs_ref, wo_ref, o_ref, *, tm):
    g = pl.program_id(1)

    @pl.when(g == 0)
    def _():
        o_ref[...] = x_ref[...]

    inv = lax.rsqrt(jnp.sum(ssq_ref[...], axis=1, keepdims=True) / SGU_DIM + RMS_EPS)
    vn = (v_ref[...].astype(F32) * inv * vg_ref[...]).astype(BF16)
    row = lax.broadcasted_iota(jnp.int32, (SGU_CHUNK, SGU_CHUNK), 0)
    col = lax.broadcasted_iota(jnp.int32, (SGU_CHUNK, SGU_CHUNK), 1)
    w_s = jnp.where(row >= col, ws_ref[0], 0.0).astype(BF16)
    bias = bs_ref[0][:, :1]
    parts = []
    for c in range(tm // SGU_CHUNK):
        vc = vn[c * SGU_CHUNK:(c + 1) * SGU_CHUNK, :]
        parts.append(jnp.dot(w_s, vc, preferred_element_type=F32) + bias)
    sv = jnp.concatenate(parts, axis=0)
    p = (u_ref[...].astype(F32) * sv).astype(BF16)
    o_ref[...] += jnp.dot(p, wo_ref[...], preferred_element_type=F32)


def _sgu_out(x, z, ssq, v_gain, w_spatial, b_spatial, w_out):
    t, d = x.shape
    tm = 512
    gd = SGU_GROUP_DIM
    b_rep = jnp.broadcast_to(b_spatial[:, :, None], (SGU_GROUPS, SGU_CHUNK, LANES))
    kern = functools.partial(_sgu_out_kernel, tm=tm)
    return pl.pallas_call(
        kern,
        grid=(t // tm, SGU_GROUPS),
        in_specs=[
            pl.BlockSpec((tm, d), lambda i, g: (i, 0)),
            pl.BlockSpec((tm, gd), lambda i, g: (i, g)),
            pl.BlockSpec((tm, gd), lambda i, g: (i, SGU_GROUPS + g)),
            pl.BlockSpec((tm, LANES), lambda i, g: (i, 0)),
            pl.BlockSpec((1, gd), lambda i, g: (0, g)),
            pl.BlockSpec((1, SGU_CHUNK, SGU_CHUNK), lambda i, g: (g, 0, 0)),
            pl.BlockSpec((1, SGU_CHUNK, LANES), lambda i, g: (g, 0, 0)),
            pl.BlockSpec((gd, d), lambda i, g: (g, 0)),
        ],
        out_specs=pl.BlockSpec((tm, d), lambda i, g: (i, 0)),
        out_shape=jax.ShapeDtypeStruct((t, d), F32),
        compiler_params=_params("parallel", "arbitrary"),
        name="sgu_out",
    )(x, z, z, ssq, v_gain.reshape(1, SGU_DIM), w_spatial, b_rep, w_out)


def kernel(x, ffn1_norm, ffn1_w_gate, ffn1_w_up, ffn1_w_down, mix_norm, ffn2_norm, ffn2_w_gate, ffn2_w_up, ffn2_w_down, attn_w_qkv, attn_q_gain, attn_k_gain, attn_w_out, pool_w_group, pool_scale, sgu_w_in, sgu_v_gain, sgu_w_spatial, sgu_b_spatial, sgu_w_out):
    batch, seq, d = x.shape
    assert d == D_MODEL and seq % MOBA_BLOCK == 0 and seq % SGU_CHUNK == 0
    xt = x.reshape(batch * seq, d)
    for i in range(DEPTH):
        xt = _ffn(xt, ffn1_norm[i], *_prep_ffn_weights(ffn1_w_gate[i], ffn1_w_up[i], ffn1_w_down[i]))
        kind, j = i % N_MIXERS, i // N_MIXERS
        if kind == 0:
            q, k, v = _qkv_proj(xt, mix_norm[i], attn_w_qkv[j].astype(BF16), attn_q_gain[j], attn_k_gain[j])
            o = _moba_attention(q, k, v, batch, seq)
            xt = _proj_residual(xt, o, attn_w_out[j].astype(BF16))
        elif kind == 1:
            xt = _pool_mixer(xt, mix_norm[i], pool_w_group[j], pool_scale[j], batch, seq)
        else:
            z, ssq = _sgu_in(xt, mix_norm[i], sgu_w_in[j].astype(BF16))
            xt = _sgu_out(xt, z, ssq, sgu_v_gain[j], sgu_w_spatial[j], sgu_b_spatial[j],
                          sgu_w_out[j].astype(BF16))
        xt = _ffn(xt, ffn2_norm[i], *_prep_ffn_weights(ffn2_w_gate[i], ffn2_w_up[i], ffn2_w_down[i]))
    return xt.reshape(batch, seq, d)
```

```python
import functools

import jax
import jax.numpy as jnp
import numpy as np
from jax import lax
from jax.experimental import pallas as pl
from jax.experimental.pallas import tpu as pltpu

D_MODEL = 2048
DEPTH = 4
N_MIXERS = 3
RMS_EPS = 1e-6
D_FF = 5504
N_HEADS = 16
HEAD_DIM = D_MODEL // N_HEADS
MOBA_BLOCK = 256
MOBA_TOPK = 3
POOL_WINDOWS = (2, 4, 8, 16)
N_POOL_GROUPS = len(POOL_WINDOWS)
POOL_GROUP_DIM = D_MODEL // N_POOL_GROUPS
POOL_HALO = max(POOL_WINDOWS)
SGU_DIM = 3 * D_MODEL
SGU_CHUNK = 128
SGU_GROUPS = 8
SGU_GROUP_DIM = SGU_DIM // SGU_GROUPS

LANES = 128
MXU_DIM = 256
VMEM_LIMIT = 52 * 1024 * 1024

F32 = jnp.float32
BF16 = jnp.bfloat16
MASK_BIAS = -1e9
LOG2E = float(np.log2(np.e))
ATTN_GROUP = 4
ATTN_HEADS_PER_STEP = 2
NT_DIMS = (((1,), (1,)), ((), ()))


def _round_up(n, m):
    return -(-n // m) * m


D_FF_PAD = _round_up(D_FF, 2 * MXU_DIM)
FFN_TM = 512
FFN_TF = 512


def _params(*sem):
    return pltpu.CompilerParams(dimension_semantics=sem, vmem_limit_bytes=VMEM_LIMIT)


def _rms_rows(x, gain):
    ms = jnp.mean(x * x, axis=-1, keepdims=True)
    return x * lax.rsqrt(ms + RMS_EPS) * gain


def _ffn_kernel(x_ref, g_ref, wg_ref, wu_ref, wd_ref, o_ref, h_ref):
    @pl.when(pl.program_id(1) == 0)
    def _():
        x = x_ref[...]
        h_ref[...] = _rms_rows(x, g_ref[...]).astype(BF16)
        o_ref[...] = x

    h = h_ref[...]
    gate = jnp.dot(h, wg_ref[...], preferred_element_type=F32)
    up = jnp.dot(h, wu_ref[...], preferred_element_type=F32)
    a = (gate * jax.nn.sigmoid(gate)) * up * 0.5
    o_ref[...] += jnp.dot(a.astype(BF16), wd_ref[...], preferred_element_type=F32)


def _ffn(x, gain, wg, wu, wd, layer):
    t, d = x.shape
    tm, tf = FFN_TM, FFN_TF
    return pl.pallas_call(
        _ffn_kernel,
        grid=(t // tm, D_FF_PAD // tf),
        in_specs=[
            pl.BlockSpec((tm, d), lambda i, f: (i, 0)),
            pl.BlockSpec((1, d), lambda i, f: (0, 0)),
            pl.BlockSpec((None, d, tf), lambda i, f: (layer, 0, f)),
            pl.BlockSpec((None, d, tf), lambda i, f: (layer, 0, f)),
            pl.BlockSpec((None, tf, d), lambda i, f: (layer, f, 0)),
        ],
        out_specs=pl.BlockSpec((tm, d), lambda i, f: (i, 0)),
        out_shape=jax.ShapeDtypeStruct((t, d), F32),
        scratch_shapes=[pltpu.VMEM((tm, d), BF16)],
        compiler_params=_params("parallel", "arbitrary"),
        name="ffn",
    )(x, gain.reshape(1, d), wg, wu, wd)


def _prep_ffn_weights(w_gate, w_up, w_down):
    pad = D_FF_PAD - D_FF
    wg = jnp.pad(w_gate.astype(BF16), ((0, 0), (0, 0), (0, pad)))
    wu = jnp.pad(w_up.astype(BF16), ((0, 0), (0, 0), (0, pad)))
    wd = jnp.pad(w_down.astype(BF16), ((0, 0), (0, pad), (0, 0)))
    return wg, wu, wd


def _qkv_kernel(x_ref, g_ref, w_ref, qg_ref, kg_ref, q_ref, k_ref, v_ref, h_ref, *, n_q):
    j = pl.program_id(1)

    @pl.when(j == 0)
    def _():
        h_ref[...] = _rms_rows(x_ref[...], g_ref[...]).astype(BF16)

    y = jnp.dot(h_ref[...], w_ref[...], preferred_element_type=F32)
    heads = y.shape[1] // HEAD_DIM

    def head_norm(gain):
        cols = []
        for hh in range(heads):
            yh = y[:, hh * HEAD_DIM:(hh + 1) * HEAD_DIM]
            cols.append(_rms_rows(yh, gain))
        return jnp.concatenate(cols, axis=1)

    @pl.when(j < n_q)
    def _():
        q_ref[...] = head_norm(qg_ref[...])

    @pl.when(jnp.logical_and(j >= n_q, j < 2 * n_q))
    def _():
        k_ref[...] = head_norm(kg_ref[...]).astype(BF16)

    @pl.when(j >= 2 * n_q)
    def _():
        v_ref[...] = y.astype(BF16)


def _qkv_proj(x, gain, w, layer, q_gain, k_gain):
    t, d = x.shape
    tm, tn = 512, 512
    n_q = d // tn
    kern = functools.partial(_qkv_kernel, n_q=n_q)
    return pl.pallas_call(
        kern,
        grid=(t // tm, 3 * n_q),
        in_specs=[
            pl.BlockSpec((tm, d), lambda i, j: (i, 0)),
            pl.BlockSpec((1, d), lambda i, j: (0, 0)),
            pl.BlockSpec((None, d, tn), lambda i, j: (layer, 0, j)),
            pl.BlockSpec((1, HEAD_DIM), lambda i, j: (0, 0)),
            pl.BlockSpec((1, HEAD_DIM), lambda i, j: (0, 0)),
        ],
        out_specs=[
            pl.BlockSpec((tm, tn), lambda i, j: (i, jnp.minimum(j, n_q - 1))),
            pl.BlockSpec((tm, tn), lambda i, j: (i, jnp.clip(j - n_q, 0, n_q - 1))),
            pl.BlockSpec((tm, tn), lambda i, j: (i, jnp.maximum(j - 2 * n_q, 0))),
        ],
        out_shape=[
            jax.ShapeDtypeStruct((t, d), F32),
            jax.ShapeDtypeStruct((t, d), BF16),
            jax.ShapeDtypeStruct((t, d), BF16),
        ],
        scratch_shapes=[pltpu.VMEM((tm, d), BF16)],
        compiler_params=_params("parallel", "arbitrary"),
        name="qkv_proj",
    )(x, gain.reshape(1, d), w, q_gain.reshape(1, HEAD_DIM), k_gain.reshape(1, HEAD_DIM))


def _attn_step(n_past, group, hh, i, slope_ref, q_ref, k_ref, kt_ref, va_ref, kmean_ref, o_ref):
    bs = MOBA_BLOCK
    c1 = HEAD_DIM ** -0.5 * LOG2E
    hcols = slice(hh * HEAD_DIM, (hh + 1) * HEAD_DIM)
    q32 = q_ref[:, hcols]
    qb = q32.astype(BF16)
    slope2 = slope_ref[hh] * LOG2E
    key_off = lax.broadcasted_iota(jnp.int32, (1, bs), 1)

    def col_bias(j):
        return slope2 * ((j - i) * bs + key_off).astype(F32)

    row = lax.broadcasted_iota(jnp.int32, (bs, bs), 0)
    col = lax.broadcasted_iota(jnp.int32, (bs, bs), 1)
    k_own = k_ref[pl.ds(pl.multiple_of(i * bs, bs), bs), hcols]
    s = lax.dot_general(qb, k_own, NT_DIMS, preferred_element_type=F32)
    t = jnp.where(row >= col, s * c1 + col_bias(i), -jnp.inf)
    m = jnp.max(jnp.maximum(t[:, :LANES], t[:, LANES:]), axis=1, keepdims=True)
    acc = jnp.dot(jnp.exp2(t - m).astype(BF16), va_ref[hh, pl.ds(pl.multiple_of(i * bs, bs), bs), :],
                  preferred_element_type=F32)

    if n_past:
        gate = lax.dot_general(kmean_ref[hh, :n_past, :], q32, NT_DIMS, precision=lax.Precision.HIGHEST,
                               preferred_element_type=F32)
        blk = lax.broadcasted_iota(jnp.int32, gate.shape, 0)
        rank = jnp.zeros(gate.shape, jnp.int32)
        for jp in range(n_past):
            gj = gate[jp:jp + 1, :]
            beats = jnp.logical_or(gj > gate, jnp.logical_and(gj == gate, blk > jp))
            rank = rank + jnp.where(jnp.logical_and(beats, jp < i), 1, 0)
        selected = jnp.logical_and(blk < i, rank < MOBA_TOPK)
        bias_t = jnp.where(selected, 0.0, MASK_BIAS).astype(F32)
        bias_t = jnp.concatenate([bias_t, jnp.zeros((LANES - n_past, bs), F32)], axis=0)
        q_aug = jnp.concatenate([qb, bias_t.T.astype(BF16)], axis=1)
        for j0 in range(0, n_past, group):
            j1 = min(j0 + group, n_past)
            s = jnp.dot(q_aug, kt_ref[hh, :, j0 * bs:j1 * bs], preferred_element_type=F32)
            ts = [s[:, (j - j0) * bs:(j - j0 + 1) * bs] * c1 + col_bias(j) for j in range(j0, j1)]
            tmax = functools.reduce(jnp.maximum, ts)
            m_blk = jnp.max(jnp.maximum(tmax[:, :LANES], tmax[:, LANES:]), axis=1, keepdims=True)
            m_new = jnp.maximum(m, m_blk)
            alpha = jnp.exp2(m - m_new)
            pcat = jnp.concatenate([jnp.exp2(t - m_new).astype(BF16) for t in ts], axis=1)
            acc = alpha * acc + jnp.dot(pcat, va_ref[hh, j0 * bs:j1 * bs, :], preferred_element_type=F32)
            m = m_new

    o_ref[:, hcols] = (acc[:, :HEAD_DIM] / acc[:, HEAD_DIM:]).astype(o_ref.dtype)


def _attn_kernel(slope_ref, q_ref, k_ref, v_ref, o_ref, kt_ref, va_ref, kmean_ref, *, n_blocks, group):
    bs = MOBA_BLOCK
    i = pl.program_id(2)
    heads = q_ref.shape[1] // HEAD_DIM

    @pl.when(i == 0)
    def _():
        feat = lax.broadcasted_iota(jnp.int32, (LANES, bs), 0)
        for hh in range(heads):
            hcols = slice(hh * HEAD_DIM, (hh + 1) * HEAD_DIM)
            va_ref[hh, :, :HEAD_DIM] = v_ref[:, hcols]
            va_ref[hh, :, HEAD_DIM:] = jnp.ones((v_ref.shape[0], LANES), BF16)
            for j in range(n_blocks):
                kj = k_ref[j * bs:(j + 1) * bs, hcols].astype(F32)
                kmean_ref[hh, j:j + 1, :] = jnp.mean(kj, axis=0, keepdims=True)
                kt_ref[hh, :HEAD_DIM, j * bs:(j + 1) * bs] = kj.T.astype(BF16)
                kt_ref[hh, HEAD_DIM:, j * bs:(j + 1) * bs] = jnp.where(feat == j, 1.0, 0.0).astype(BF16)

    n_groups = (i + group - 1) // group
    for c in range((n_blocks - 1 + group - 1) // group + 1):
        n_past = min(c * group, n_blocks - 1)

        @pl.when(n_groups == c)
        def _(n_past=n_past):
            for hh in range(heads):
                _attn_step(n_past, group, hh, i, slope_ref, q_ref, k_ref, kt_ref, va_ref, kmean_ref, o_ref)


def _moba_attention(q, k, v, batch, seq):
    t, d = q.shape
    bs = MOBA_BLOCK
    n_blocks = seq // bs
    slopes = np.asarray(2.0 ** (-8.0 * np.arange(1, N_HEADS + 1) / N_HEADS), dtype=np.float32)
    slopes = jnp.asarray(np.broadcast_to(slopes[:, None, None], (N_HEADS, 1, bs)))
    kern = functools.partial(_attn_kernel, n_blocks=n_blocks, group=ATTN_GROUP)
    hps = ATTN_HEADS_PER_STEP
    hw = hps * HEAD_DIM
    return pl.pallas_call(
        kern,
        grid=(batch, N_HEADS // hps, n_blocks),
        in_specs=[
            pl.BlockSpec((hps, 1, bs), lambda b, h, i: (h, 0, 0)),
            pl.BlockSpec((bs, hw), lambda b, h, i: (b * n_blocks + i, h)),
            pl.BlockSpec((seq, hw), lambda b, h, i: (b, h)),
            pl.BlockSpec((seq, hw), lambda b, h, i: (b, h)),
        ],
        out_specs=pl.BlockSpec((bs, hw), lambda b, h, i: (b * n_blocks + i, h)),
        out_shape=jax.ShapeDtypeStruct((t, d), BF16),
        scratch_shapes=[
            pltpu.VMEM((hps, HEAD_DIM + LANES, seq), BF16),
            pltpu.VMEM((hps, seq, HEAD_DIM + LANES), BF16),
            pltpu.VMEM((hps, n_blocks, HEAD_DIM), F32),
        ],
        compiler_params=_params("parallel", "parallel", "arbitrary"),
        name="moba_attn",
    )(slopes, q, k, v)


def _proj_res_kernel(x_ref, a_ref, w_ref, o_ref):
    o_ref[...] = x_ref[...] + jnp.dot(a_ref[...], w_ref[...], preferred_element_type=F32)


def _proj_residual(x, a, w, layer):
    t, n = x.shape
    kdim = a.shape[1]
    tm, tn = 512, 1024
    return pl.pallas_call(
        _proj_res_kernel,
        grid=(t // tm, n // tn),
        in_specs=[
            pl.BlockSpec((tm, tn), lambda i, j: (i, j)),
            pl.BlockSpec((tm, kdim), lambda i, j: (i, 0)),
            pl.BlockSpec((None, kdim, tn), lambda i, j: (layer, 0, j)),
        ],
        out_specs=pl.BlockSpec((tm, tn), lambda i, j: (i, j)),
        out_shape=jax.ShapeDtypeStruct((t, n), F32),
        compiler_params=_params("parallel", "parallel"),
        name="proj_residual",
    )(x, a, w)


def _pool_kernel(x_ref, halo_ref, g_ref, w_ref, sc_ref, o_ref, *, ts):
    i = pl.program_id(1)
    x = x_ref[...]
    gain = g_ref[...]
    h = _rms_rows(x, gain)
    h_halo = jnp.where(i == 0, 0.0, _rms_rows(halo_ref[...], gain))
    he = jnp.concatenate([h_halo, h], axis=0)
    pos = i * ts + lax.broadcasted_iota(jnp.int32, (ts, 1), 0)
    count = (pos + 1).astype(F32)
    cg = POOL_GROUP_DIM
    outs = []
    for g, win in enumerate(POOL_WINDOWS):
        a = he[:, g * cg:(g + 1) * cg]
        lead = 0
        width = 1
        while width < win:
            a = a[width:, :] + a[:-width, :]
            lead += width
            width *= 2
        start = POOL_HALO - lead
        pooled = a[start:start + ts, :] / jnp.minimum(count, float(win)) - h[:, g * cg:(g + 1) * cg]
        outs.append(jnp.dot(pooled.astype(BF16), w_ref[g], preferred_element_type=F32))
    y = jnp.concatenate(outs, axis=1)
    o_ref[...] = x + y * sc_ref[...]


def _pool_mixer(x, gain, w_group, scale, batch, seq):
    t, d = x.shape
    ts = 512
    per_seq = seq // ts
    halo_per_tile = ts // POOL_HALO
    kern = functools.partial(_pool_kernel, ts=ts)

    def halo_map(b, i):
        return (jnp.maximum((b * per_seq + i) * halo_per_tile - 1, 0), 0)

    return pl.pallas_call(
        kern,
        grid=(batch, per_seq),
        in_specs=[
            pl.BlockSpec((ts, d), lambda b, i: (b * per_seq + i, 0)),
            pl.BlockSpec((POOL_HALO, d), halo_map),
            pl.BlockSpec((1, d), lambda b, i: (0, 0)),
            pl.BlockSpec((N_POOL_GROUPS, POOL_GROUP_DIM, POOL_GROUP_DIM), lambda b, i: (0, 0, 0)),
            pl.BlockSpec((1, d), lambda b, i: (0, 0)),
        ],
        out_specs=pl.BlockSpec((ts, d), lambda b, i: (b * per_seq + i, 0)),
        out_shape=jax.ShapeDtypeStruct((t, d), F32),
        compiler_params=_params("parallel", "parallel"),
        name="pool_mixer",
    )(x, x, gain.reshape(1, d), w_group.astype(BF16), scale.reshape(1, d))


def _sgu_in_kernel(x_ref, g_ref, w_ref, z_ref, ssq_ref, h_ref, *, n_u):
    j = pl.program_id(1)

    @pl.when(j == 0)
    def _():
        h_ref[...] = _rms_rows(x_ref[...], g_ref[...]).astype(BF16)
        ssq_ref[...] = jnp.zeros_like(ssq_ref)

    y = jnp.dot(h_ref[...], w_ref[...], preferred_element_type=F32)
    z = 0.5 * y * (1.0 + lax.erf(y * np.float32(np.sqrt(0.5))))
    z_ref[...] = z.astype(BF16)

    @pl.when(j >= n_u)
    def _():
        zz = z * z
        part = zz[:, :LANES]
        for c in range(1, zz.shape[1] // LANES):
            part = part + zz[:, c * LANES:(c + 1) * LANES]
        ssq_ref[...] += part


def _sgu_in(x, gain, w_in, layer):
    t, d = x.shape
    n = w_in.shape[2]
    tm, tn = 512, 512
    n_u = (n // 2) // tn
    kern = functools.partial(_sgu_in_kernel, n_u=n_u)
    return pl.pallas_call(
        kern,
        grid=(t // tm, n // tn),
        in_specs=[
            pl.BlockSpec((tm, d), lambda i, j: (i, 0)),
            pl.BlockSpec((1, d), lambda i, j: (0, 0)),
            pl.BlockSpec((None, d, tn), lambda i, j: (layer, 0, j)),
        ],
        out_specs=[
            pl.BlockSpec((tm, tn), lambda i, j: (i, j)),
            pl.BlockSpec((tm, LANES), lambda i, j: (i, 0)),
        ],
        out_shape=[
            jax.ShapeDtypeStruct((t, n), BF16),
            jax.ShapeDtypeStruct((t, LANES), F32),
        ],
        scratch_shapes=[pltpu.VMEM((tm, d), BF16)],
        compiler_params=_params("parallel", "arbitrary"),
        name="sgu_in",
    )(x, gain.reshape(1, d), w_in)


def _sgu_out_kernel(x_ref, u_ref, v_ref, ssq_ref, vg_ref, ws_ref, bs_ref, wo_ref, o_ref, *, tm):
    g = pl.program_id(1)

    @pl.when(g == 0)
    def _():
        o_ref[...] = x_ref[...]

    inv = lax.rsqrt(jnp.sum(ssq_ref[...], axis=1, keepdims=True) / SGU_DIM + RMS_EPS)
    vn = (v_ref[...].astype(F32) * inv * vg_ref[...]).astype(BF16)
    row = lax.broadcasted_iota(jnp.int32, (SGU_CHUNK, SGU_CHUNK), 0)
    col = lax.broadcasted_iota(jnp.int32, (SGU_CHUNK, SGU_CHUNK), 1)
    w_s = jnp.where(row >= col, ws_ref[0], 0.0).astype(BF16)
    bias = bs_ref[0][:, :1]
    parts = []
    for c in range(tm // SGU_CHUNK):
        vc = vn[c * SGU_CHUNK:(c + 1) * SGU_CHUNK, :]
        parts.append(jnp.dot(w_s, vc, preferred_element_type=F32) + bias)
    sv = jnp.concatenate(parts, axis=0)
    p = (u_ref[...].astype(F32) * sv).astype(BF16)
    o_ref[...] += jnp.dot(p, wo_ref[...], preferred_element_type=F32)


def _sgu_out(x, z, ssq, v_gain, w_spatial, b_spatial, w_out, layer):
    t, d = x.shape
    tm = 512
    gd = SGU_GROUP_DIM
    b_rep = jnp.broadcast_to(b_spatial[:, :, None], (SGU_GROUPS, SGU_CHUNK, LANES))
    kern = functools.partial(_sgu_out_kernel, tm=tm)
    return pl.pallas_call(
        kern,
        grid=(t // tm, SGU_GROUPS),
        in_specs=[
            pl.BlockSpec((tm, d), lambda i, g: (i, 0)),
            pl.BlockSpec((tm, gd), lambda i, g: (i, g)),
            pl.BlockSpec((tm, gd), lambda i, g: (i, SGU_GROUPS + g)),
            pl.BlockSpec((tm, LANES), lambda i, g: (i, 0)),
            pl.BlockSpec((1, gd), lambda i, g: (0, g)),
            pl.BlockSpec((1, SGU_CHUNK, SGU_CHUNK), lambda i, g: (g, 0, 0)),
            pl.BlockSpec((1, SGU_CHUNK, LANES), lambda i, g: (g, 0, 0)),
            pl.BlockSpec((None, gd, d), lambda i, g: (layer, g, 0)),
        ],
        out_specs=pl.BlockSpec((tm, d), lambda i, g: (i, 0)),
        out_shape=jax.ShapeDtypeStruct((t, d), F32),
        compiler_params=_params("parallel", "arbitrary"),
        name="sgu_out",
    )(x, z, z, ssq, v_gain.reshape(1, SGU_DIM), w_spatial, b_rep, w_out)


def kernel(x, ffn1_norm, ffn1_w_gate, ffn1_w_up, ffn1_w_down, mix_norm, ffn2_norm, ffn2_w_gate, ffn2_w_up, ffn2_w_down, attn_w_qkv, attn_q_gain, attn_k_gain, attn_w_out, pool_w_group, pool_scale, sgu_w_in, sgu_v_gain, sgu_w_spatial, sgu_b_spatial, sgu_w_out):
    batch, seq, d = x.shape
    assert d == D_MODEL and seq % MOBA_BLOCK == 0 and seq % SGU_CHUNK == 0
    xt = x.reshape(batch * seq, d)
    ffn1 = _prep_ffn_weights(ffn1_w_gate, ffn1_w_up, ffn1_w_down)
    ffn2 = _prep_ffn_weights(ffn2_w_gate, ffn2_w_up, ffn2_w_down)
    w_qkv, w_attn_out = attn_w_qkv.astype(BF16), attn_w_out.astype(BF16)
    w_sgu_in, w_sgu_out = sgu_w_in.astype(BF16), sgu_w_out.astype(BF16)
    for i in range(DEPTH):
        xt = _ffn(xt, ffn1_norm[i], *ffn1, i)
        kind, j = i % N_MIXERS, i // N_MIXERS
        if kind == 0:
            q, k, v = _qkv_proj(xt, mix_norm[i], w_qkv, j, attn_q_gain[j], attn_k_gain[j])
            o = _moba_attention(q, k, v, batch, seq)
            xt = _proj_residual(xt, o, w_attn_out, j)
        elif kind == 1:
            xt = _pool_mixer(xt, mix_norm[i], pool_w_group[j], pool_scale[j], batch, seq)
        else:
            z, ssq = _sgu_in(xt, mix_norm[i], w_sgu_in, j)
            xt = _sgu_out(xt, z, ssq, sgu_v_gain[j], sgu_w_spatial[j], sgu_b_spatial[j], w_sgu_out, j)
        xt = _ffn(xt, ffn2_norm[i], *ffn2, i)
    return xt.reshape(batch, seq, d)
```

```python
import functools

import jax
import jax.numpy as jnp
import numpy as np
from jax import lax
from jax.experimental import pallas as pl
from jax.experimental.pallas import tpu as pltpu

D_MODEL = 2048
DEPTH = 4
N_MIXERS = 3
RMS_EPS = 1e-6
D_FF = 5504
N_HEADS = 16
HEAD_DIM = D_MODEL // N_HEADS
MOBA_BLOCK = 256
MOBA_TOPK = 3
POOL_WINDOWS = (2, 4, 8, 16)
N_POOL_GROUPS = len(POOL_WINDOWS)
POOL_GROUP_DIM = D_MODEL // N_POOL_GROUPS
POOL_HALO = max(POOL_WINDOWS)
SGU_DIM = 3 * D_MODEL
SGU_CHUNK = 128
SGU_GROUPS = 8
SGU_GROUP_DIM = SGU_DIM // SGU_GROUPS

LANES = 128
MXU_DIM = 256
VMEM_LIMIT = 56 * 1024 * 1024

F32 = jnp.float32
BF16 = jnp.bfloat16
MASK_BIAS = -1e9
LOG2E = float(np.log2(np.e))
ATTN_GROUP = 4
ATTN_HEADS_PER_STEP = 2
NT_DIMS = (((1,), (1,)), ((), ()))


SUB_N = 2 * MXU_DIM
FFN_TM = 1024
FFN_TF = 512


def _params(*sem):
    return pltpu.CompilerParams(dimension_semantics=sem, vmem_limit_bytes=VMEM_LIMIT)


def _rms_rows(x, gain):
    ms = jnp.mean(x * x, axis=-1, keepdims=True)
    return x * lax.rsqrt(ms + RMS_EPS) * gain


def _ffn_kernel(x_ref, g_ref, wg_ref, wu_ref, wd_ref, o_ref, h_ref):
    f = pl.program_id(1)
    tf = wd_ref.shape[0]

    @pl.when(f == 0)
    def _():
        x = x_ref[...]
        h_ref[...] = _rms_rows(x, g_ref[...]).astype(BF16)
        o_ref[...] = x

    h = h_ref[...]
    gate = jnp.dot(h, wg_ref[...], preferred_element_type=F32)
    up = jnp.dot(h, wu_ref[...], preferred_element_type=F32)
    a = (gate * jax.nn.sigmoid(gate)) * up * 0.5
    valid = D_FF - f * tf
    a = jnp.where(lax.broadcasted_iota(jnp.int32, (1, tf), 1) < valid, a, 0.0)
    wd = wd_ref[...]
    wd = jnp.where(lax.broadcasted_iota(jnp.int32, (tf, 1), 0) < valid, wd, jnp.zeros_like(wd))
    o_ref[...] += jnp.dot(a.astype(BF16), wd, preferred_element_type=F32)


def _ffn(x, gain, wg, wu, wd, layer):
    t, d = x.shape
    tm, tf = FFN_TM, FFN_TF
    return pl.pallas_call(
        _ffn_kernel,
        grid=(t // tm, pl.cdiv(D_FF, tf)),
        in_specs=[
            pl.BlockSpec((tm, d), lambda i, f: (i, 0)),
            pl.BlockSpec((1, d), lambda i, f: (0, 0)),
            pl.BlockSpec((None, d, tf), lambda i, f: (layer, 0, f)),
            pl.BlockSpec((None, d, tf), lambda i, f: (layer, 0, f)),
            pl.BlockSpec((None, tf, d), lambda i, f: (layer, f, 0)),
        ],
        out_specs=pl.BlockSpec((tm, d), lambda i, f: (i, 0)),
        out_shape=jax.ShapeDtypeStruct((t, d), F32),
        scratch_shapes=[pltpu.VMEM((tm, d), BF16)],
        compiler_params=_params("parallel", "arbitrary"),
        name="ffn",
    )(x, gain.reshape(1, d), wg, wu, wd)


def _qkv_kernel(x_ref, g_ref, w_ref, qg_ref, kg_ref, q_ref, k_ref, v_ref, h_ref):
    j = pl.program_id(1)

    @pl.when(j == 0)
    def _():
        h_ref[...] = _rms_rows(x_ref[...], g_ref[...]).astype(BF16)

    def head_norm(y, gain):
        cols = []
        for hh in range(y.shape[1] // HEAD_DIM):
            cols.append(_rms_rows(y[:, hh * HEAD_DIM:(hh + 1) * HEAD_DIM], gain))
        return jnp.concatenate(cols, axis=1)

    def project(out_ref, epilogue):
        h = h_ref[...]
        for c0 in range(0, w_ref.shape[1], SUB_N):
            y = jnp.dot(h, w_ref[:, c0:c0 + SUB_N], preferred_element_type=F32)
            out_ref[:, c0:c0 + SUB_N] = epilogue(y).astype(out_ref.dtype)

    @pl.when(j == 0)
    def _():
        project(q_ref, lambda y: head_norm(y, qg_ref[...]))

    @pl.when(j == 1)
    def _():
        project(k_ref, lambda y: head_norm(y, kg_ref[...]))

    @pl.when(j == 2)
    def _():
        project(v_ref, lambda y: y)


def _qkv_proj(x, gain, w, layer, q_gain, k_gain):
    t, d = x.shape
    tm = 512
    return pl.pallas_call(
        _qkv_kernel,
        grid=(t // tm, 3),
        in_specs=[
            pl.BlockSpec((tm, d), lambda i, j: (i, 0)),
            pl.BlockSpec((1, d), lambda i, j: (0, 0)),
            pl.BlockSpec((None, d, d), lambda i, j: (layer, 0, j)),
            pl.BlockSpec((1, HEAD_DIM), lambda i, j: (0, 0)),
            pl.BlockSpec((1, HEAD_DIM), lambda i, j: (0, 0)),
        ],
        out_specs=[
            pl.BlockSpec((tm, d), lambda i, j: (i, 0)),
            pl.BlockSpec((tm, d), lambda i, j: (i, 0)),
            pl.BlockSpec((tm, d), lambda i, j: (i, 0)),
        ],
        out_shape=[
            jax.ShapeDtypeStruct((t, d), F32),
            jax.ShapeDtypeStruct((t, d), BF16),
            jax.ShapeDtypeStruct((t, d), BF16),
        ],
        scratch_shapes=[pltpu.VMEM((tm, d), BF16)],
        compiler_params=_params("parallel", "arbitrary"),
        name="qkv_proj",
    )(x, gain.reshape(1, d), w, q_gain.reshape(1, HEAD_DIM), k_gain.reshape(1, HEAD_DIM))


def _attn_step(n_past, group, hh, i, slope_ref, q_ref, k_ref, kt_ref, va_ref, kmean_ref, o_ref):
    bs = MOBA_BLOCK
    c1 = HEAD_DIM ** -0.5 * LOG2E
    hcols = slice(hh * HEAD_DIM, (hh + 1) * HEAD_DIM)
    q32 = q_ref[:, hcols]
    qb = q32.astype(BF16)
    slope2 = slope_ref[hh] * LOG2E
    key_off = lax.broadcasted_iota(jnp.int32, (1, bs), 1)

    def col_bias(j):
        return slope2 * ((j - i) * bs + key_off).astype(F32)

    row = lax.broadcasted_iota(jnp.int32, (bs, bs), 0)
    col = lax.broadcasted_iota(jnp.int32, (bs, bs), 1)
    k_own = k_ref[pl.ds(pl.multiple_of(i * bs, bs), bs), hcols]
    s = lax.dot_general(qb, k_own, NT_DIMS, preferred_element_type=F32)
    t = jnp.where(row >= col, s * c1 + col_bias(i), -jnp.inf)
    m = jnp.max(jnp.maximum(t[:, :LANES], t[:, LANES:]), axis=1, keepdims=True)
    acc = jnp.dot(jnp.exp2(t - m).astype(BF16), va_ref[hh, pl.ds(pl.multiple_of(i * bs, bs), bs), :],
                  preferred_element_type=F32)

    if n_past:
        gate = lax.dot_general(kmean_ref[hh, :n_past, :], q32, NT_DIMS, precision=lax.Precision.HIGHEST,
                               preferred_element_type=F32)
        blk = lax.broadcasted_iota(jnp.int32, gate.shape, 0)
        rank = jnp.zeros(gate.shape, jnp.int32)
        for jp in range(n_past):
            gj = gate[jp:jp + 1, :]
            beats = jnp.logical_or(gj > gate, jnp.logical_and(gj == gate, blk > jp))
            rank = rank + jnp.where(jnp.logical_and(beats, jp < i), 1, 0)
        selected = jnp.logical_and(blk < i, rank < MOBA_TOPK)
        bias_t = jnp.where(selected, 0.0, MASK_BIAS).astype(F32)
        bias_t = jnp.concatenate([bias_t, jnp.zeros((LANES - n_past, bs), F32)], axis=0)
        q_aug = jnp.concatenate([qb, bias_t.T.astype(BF16)], axis=1)
        for j0 in range(0, n_past, group):
            j1 = min(j0 + group, n_past)
            s = jnp.dot(q_aug, kt_ref[hh, :, j0 * bs:j1 * bs], preferred_element_type=F32)
            ts = [s[:, (j - j0) * bs:(j - j0 + 1) * bs] * c1 + col_bias(j) for j in range(j0, j1)]
            tmax = functools.reduce(jnp.maximum, ts)
            m_blk = jnp.max(jnp.maximum(tmax[:, :LANES], tmax[:, LANES:]), axis=1, keepdims=True)
            m_new = jnp.maximum(m, m_blk)
            alpha = jnp.exp2(m - m_new)
            pcat = jnp.concatenate([jnp.exp2(t - m_new).astype(BF16) for t in ts], axis=1)
            acc = alpha * acc + jnp.dot(pcat, va_ref[hh, j0 * bs:j1 * bs, :], preferred_element_type=F32)
            m = m_new

    o_ref[:, hcols] = (acc[:, :HEAD_DIM] / acc[:, HEAD_DIM:]).astype(o_ref.dtype)


def _attn_kernel(slope_ref, q_ref, k_ref, v_ref, o_ref, kt_ref, va_ref, kmean_ref, *, n_blocks, group):
    bs = MOBA_BLOCK
    i = pl.program_id(2)
    heads = q_ref.shape[1] // HEAD_DIM

    @pl.when(i == 0)
    def _():
        feat = lax.broadcasted_iota(jnp.int32, (LANES, bs), 0)
        for hh in range(heads):
            hcols = slice(hh * HEAD_DIM, (hh + 1) * HEAD_DIM)
            va_ref[hh, :, :HEAD_DIM] = v_ref[:, hcols]
            va_ref[hh, :, HEAD_DIM:] = jnp.ones((v_ref.shape[0], LANES), BF16)
            for j in range(n_blocks):
                kj = k_ref[j * bs:(j + 1) * bs, hcols].astype(F32)
                kmean_ref[hh, j:j + 1, :] = jnp.mean(kj, axis=0, keepdims=True)
                kt_ref[hh, :HEAD_DIM, j * bs:(j + 1) * bs] = kj.T.astype(BF16)
                kt_ref[hh, HEAD_DIM:, j * bs:(j + 1) * bs] = jnp.where(feat == j, 1.0, 0.0).astype(BF16)

    n_groups = (i + group - 1) // group
    for c in range((n_blocks - 1 + group - 1) // group + 1):
        n_past = min(c * group, n_blocks - 1)

        @pl.when(n_groups == c)
        def _(n_past=n_past):
            for hh in range(heads):
                _attn_step(n_past, group, hh, i, slope_ref, q_ref, k_ref, kt_ref, va_ref, kmean_ref, o_ref)


def _moba_attention(q, k, v, batch, seq):
    t, d = q.shape
    bs = MOBA_BLOCK
    n_blocks = seq // bs
    slopes = np.asarray(2.0 ** (-8.0 * np.arange(1, N_HEADS + 1) / N_HEADS), dtype=np.float32)
    slopes = jnp.asarray(np.broadcast_to(slopes[:, None, None], (N_HEADS, 1, bs)))
    kern = functools.partial(_attn_kernel, n_blocks=n_blocks, group=ATTN_GROUP)
    hps = ATTN_HEADS_PER_STEP
    hw = hps * HEAD_DIM
    return pl.pallas_call(
        kern,
        grid=(batch, N_HEADS // hps, n_blocks),
        in_specs=[
            pl.BlockSpec((hps, 1, bs), lambda b, h, i: (h, 0, 0)),
            pl.BlockSpec((bs, hw), lambda b, h, i: (b * n_blocks + i, h)),
            pl.BlockSpec((seq, hw), lambda b, h, i: (b, h)),
            pl.BlockSpec((seq, hw), lambda b, h, i: (b, h)),
        ],
        out_specs=pl.BlockSpec((bs, hw), lambda b, h, i: (b * n_blocks + i, h)),
        out_shape=jax.ShapeDtypeStruct((t, d), BF16),
        scratch_shapes=[
            pltpu.VMEM((hps, HEAD_DIM + LANES, seq), BF16),
            pltpu.VMEM((hps, seq, HEAD_DIM + LANES), BF16),
            pltpu.VMEM((hps, n_blocks, HEAD_DIM), F32),
        ],
        compiler_params=_params("parallel", "parallel", "arbitrary"),
        name="moba_attn",
    )(slopes, q, k, v)


def _proj_res_kernel(x_ref, a_ref, w_ref, o_ref):
    o_ref[...] = x_ref[...] + jnp.dot(a_ref[...], w_ref[...], preferred_element_type=F32)


def _proj_residual(x, a, w, layer):
    t, n = x.shape
    kdim = a.shape[1]
    tm, tn = 512, 1024
    return pl.pallas_call(
        _proj_res_kernel,
        grid=(t // tm, n // tn),
        in_specs=[
            pl.BlockSpec((tm, tn), lambda i, j: (i, j)),
            pl.BlockSpec((tm, kdim), lambda i, j: (i, 0)),
            pl.BlockSpec((None, kdim, tn), lambda i, j: (layer, 0, j)),
        ],
        out_specs=pl.BlockSpec((tm, tn), lambda i, j: (i, j)),
        out_shape=jax.ShapeDtypeStruct((t, n), F32),
        compiler_params=_params("parallel", "parallel"),
        name="proj_residual",
    )(x, a, w)


def _pool_kernel(x_ref, halo_ref, g_ref, w_ref, sc_ref, o_ref, *, ts):
    i = pl.program_id(1)
    x = x_ref[...]
    gain = g_ref[...]
    h = _rms_rows(x, gain)
    h_halo = jnp.where(i == 0, 0.0, _rms_rows(halo_ref[...], gain))
    he = jnp.concatenate([h_halo, h], axis=0)
    pos = i * ts + lax.broadcasted_iota(jnp.int32, (ts, 1), 0)
    count = (pos + 1).astype(F32)
    cg = POOL_GROUP_DIM
    outs = []
    for g, win in enumerate(POOL_WINDOWS):
        a = he[:, g * cg:(g + 1) * cg]
        lead = 0
        width = 1
        while width < win:
            a = a[width:, :] + a[:-width, :]
            lead += width
            width *= 2
        start = POOL_HALO - lead
        pooled = a[start:start + ts, :] / jnp.minimum(count, float(win)) - h[:, g * cg:(g + 1) * cg]
        outs.append(jnp.dot(pooled.astype(BF16), w_ref[g], preferred_element_type=F32))
    y = jnp.concatenate(outs, axis=1)
    o_ref[...] = x + y * sc_ref[...]


def _pool_mixer(x, gain, w_group, scale, batch, seq):
    t, d = x.shape
    ts = 512
    per_seq = seq // ts
    halo_per_tile = ts // POOL_HALO
    kern = functools.partial(_pool_kernel, ts=ts)

    def halo_map(b, i):
        return (jnp.maximum((b * per_seq + i) * halo_per_tile - 1, 0), 0)

    return pl.pallas_call(
        kern,
        grid=(batch, per_seq),
        in_specs=[
            pl.BlockSpec((ts, d), lambda b, i: (b * per_seq + i, 0)),
            pl.BlockSpec((POOL_HALO, d), halo_map),
            pl.BlockSpec((1, d), lambda b, i: (0, 0)),
            pl.BlockSpec((N_POOL_GROUPS, POOL_GROUP_DIM, POOL_GROUP_DIM), lambda b, i: (0, 0, 0)),
            pl.BlockSpec((1, d), lambda b, i: (0, 0)),
        ],
        out_specs=pl.BlockSpec((ts, d), lambda b, i: (b * per_seq + i, 0)),
        out_shape=jax.ShapeDtypeStruct((t, d), F32),
        compiler_params=_params("parallel", "parallel"),
        name="pool_mixer",
    )(x, x, gain.reshape(1, d), w_group.astype(BF16), scale.reshape(1, d))


def _sgu_in_kernel(x_ref, g_ref, w_ref, z_ref, ssq_ref, h_ref, *, n_u):
    j = pl.program_id(1)

    @pl.when(j == 0)
    def _():
        h_ref[...] = _rms_rows(x_ref[...], g_ref[...]).astype(BF16)
        ssq_ref[...] = jnp.zeros_like(ssq_ref)

    def project(with_ssq):
        h = h_ref[...]
        part = None
        for c0 in range(0, w_ref.shape[1], SUB_N):
            y = jnp.dot(h, w_ref[:, c0:c0 + SUB_N], preferred_element_type=F32)
            z = 0.5 * y * (1.0 + lax.erf(y * np.float32(np.sqrt(0.5))))
            z_ref[:, c0:c0 + SUB_N] = z.astype(BF16)
            if with_ssq:
                zz = z * z
                for c in range(SUB_N // LANES):
                    sl = zz[:, c * LANES:(c + 1) * LANES]
                    part = sl if part is None else part + sl
        if with_ssq:
            ssq_ref[...] += part

    pl.when(j < n_u)(functools.partial(project, False))
    pl.when(j >= n_u)(functools.partial(project, True))


def _sgu_in(x, gain, w_in, layer):
    t, d = x.shape
    n = w_in.shape[2]
    tm, tn = 512, 2048
    n_u = (n // 2) // tn
    kern = functools.partial(_sgu_in_kernel, n_u=n_u)
    return pl.pallas_call(
        kern,
        grid=(t // tm, n // tn),
        in_specs=[
            pl.BlockSpec((tm, d), lambda i, j: (i, 0)),
            pl.BlockSpec((1, d), lambda i, j: (0, 0)),
            pl.BlockSpec((None, d, tn), lambda i, j: (layer, 0, j)),
        ],
        out_specs=[
            pl.BlockSpec((tm, tn), lambda i, j: (i, j)),
            pl.BlockSpec((tm, LANES), lambda i, j: (i, 0)),
        ],
        out_shape=[
            jax.ShapeDtypeStruct((t, n), BF16),
            jax.ShapeDtypeStruct((t, LANES), F32),
        ],
        scratch_shapes=[pltpu.VMEM((tm, d), BF16)],
        compiler_params=_params("parallel", "arbitrary"),
        name="sgu_in",
    )(x, gain.reshape(1, d), w_in)


def _sgu_out_kernel(x_ref, u_ref, v_ref, ssq_ref, vg_ref, ws_ref, bs_ref, wo_ref, o_ref, *, tm):
    g = pl.program_id(1)

    @pl.when(g == 0)
    def _():
        o_ref[...] = x_ref[...]

    inv = lax.rsqrt(jnp.sum(ssq_ref[...], axis=1, keepdims=True) / SGU_DIM + RMS_EPS)
    vn = (v_ref[...].astype(F32) * inv * vg_ref[...]).astype(BF16)
    row = lax.broadcasted_iota(jnp.int32, (SGU_CHUNK, SGU_CHUNK), 0)
    col = lax.broadcasted_iota(jnp.int32, (SGU_CHUNK, SGU_CHUNK), 1)
    w_s = jnp.where(row >= col, ws_ref[0], 0.0).astype(BF16)
    bias = bs_ref[0][:, :1]
    parts = []
    for c in range(tm // SGU_CHUNK):
        vc = vn[c * SGU_CHUNK:(c + 1) * SGU_CHUNK, :]
        parts.append(jnp.dot(w_s, vc, preferred_element_type=F32) + bias)
    sv = jnp.concatenate(parts, axis=0)
    p = (u_ref[...].astype(F32) * sv).astype(BF16)
    o_ref[...] += jnp.dot(p, wo_ref[...], preferred_element_type=F32)


def _sgu_out(x, z, ssq, v_gain, w_spatial, b_spatial, w_out, layer):
    t, d = x.shape
    tm = 512
    gd = SGU_GROUP_DIM
    b_rep = jnp.broadcast_to(b_spatial[:, :, None], (SGU_GROUPS, SGU_CHUNK, LANES))
    kern = functools.partial(_sgu_out_kernel, tm=tm)
    return pl.pallas_call(
        kern,
        grid=(t // tm, SGU_GROUPS),
        in_specs=[
            pl.BlockSpec((tm, d), lambda i, g: (i, 0)),
            pl.BlockSpec((tm, gd), lambda i, g: (i, g)),
            pl.BlockSpec((tm, gd), lambda i, g: (i, SGU_GROUPS + g)),
            pl.BlockSpec((tm, LANES), lambda i, g: (i, 0)),
            pl.BlockSpec((1, gd), lambda i, g: (0, g)),
            pl.BlockSpec((1, SGU_CHUNK, SGU_CHUNK), lambda i, g: (g, 0, 0)),
            pl.BlockSpec((1, SGU_CHUNK, LANES), lambda i, g: (g, 0, 0)),
            pl.BlockSpec((None, gd, d), lambda i, g: (layer, g, 0)),
        ],
        out_specs=pl.BlockSpec((tm, d), lambda i, g: (i, 0)),
        out_shape=jax.ShapeDtypeStruct((t, d), F32),
        compiler_params=_params("parallel", "arbitrary"),
        name="sgu_out",
    )(x, z, z, ssq, v_gain.reshape(1, SGU_DIM), w_spatial, b_rep, w_out)


def kernel(x, ffn1_norm, ffn1_w_gate, ffn1_w_up, ffn1_w_down, mix_norm, ffn2_norm, ffn2_w_gate, ffn2_w_up, ffn2_w_down, attn_w_qkv, attn_q_gain, attn_k_gain, attn_w_out, pool_w_group, pool_scale, sgu_w_in, sgu_v_gain, sgu_w_spatial, sgu_b_spatial, sgu_w_out):
    batch, seq, d = x.shape
    assert d == D_MODEL and seq % MOBA_BLOCK == 0 and seq % SGU_CHUNK == 0
    xt = x.reshape(batch * seq, d)
    ffn1 = [w.astype(BF16) for w in (ffn1_w_gate, ffn1_w_up, ffn1_w_down)]
    ffn2 = [w.astype(BF16) for w in (ffn2_w_gate, ffn2_w_up, ffn2_w_down)]
    w_qkv, w_attn_out = attn_w_qkv.astype(BF16), attn_w_out.astype(BF16)
    w_sgu_in, w_sgu_out = sgu_w_in.astype(BF16), sgu_w_out.astype(BF16)
    for i in range(DEPTH):
        xt = _ffn(xt, ffn1_norm[i], *ffn1, i)
        kind, j = i % N_MIXERS, i // N_MIXERS
        if kind == 0:
            q, k, v = _qkv_proj(xt, mix_norm[i], w_qkv, j, attn_q_gain[j], attn_k_gain[j])
            o = _moba_attention(q, k, v, batch, seq)
            xt = _proj_residual(xt, o, w_attn_out, j)
        elif kind == 1:
            xt = _pool_mixer(xt, mix_norm[i], pool_w_group[j], pool_scale[j], batch, seq)
        else:
            z, ssq = _sgu_in(xt, mix_norm[i], w_sgu_in, j)
            xt = _sgu_out(xt, z, ssq, sgu_v_gain[j], sgu_w_spatial[j], sgu_b_spatial[j], w_sgu_out, j)
        xt = _ffn(xt, ffn2_norm[i], *ffn2, i)
    return xt.reshape(batch, seq, d)
```

```python
import functools

import jax
import jax.numpy as jnp
import numpy as np
from jax import lax
from jax.experimental import pallas as pl
from jax.experimental.pallas import tpu as pltpu

D_MODEL = 2048
DEPTH = 4
N_MIXERS = 3
RMS_EPS = 1e-6
D_FF = 5504
N_HEADS = 16
HEAD_DIM = D_MODEL // N_HEADS
MOBA_BLOCK = 256
MOBA_TOPK = 3
POOL_WINDOWS = (2, 4, 8, 16)
N_POOL_GROUPS = len(POOL_WINDOWS)
POOL_GROUP_DIM = D_MODEL // N_POOL_GROUPS
POOL_HALO = max(POOL_WINDOWS)
SGU_DIM = 3 * D_MODEL
SGU_CHUNK = 128
SGU_GROUPS = 8
SGU_GROUP_DIM = SGU_DIM // SGU_GROUPS

LANES = 128
MXU_DIM = 256
VMEM_LIMIT = 56 * 1024 * 1024

F32 = jnp.float32
BF16 = jnp.bfloat16
MASK_BIAS = -1e9
LOG2E = float(np.log2(np.e))
ATTN_GROUP = 4
ATTN_HEADS_PER_STEP = 2
SGU_GROUPS_PER_STEP = 2
NT_DIMS = (((1,), (1,)), ((), ()))


SUB_N = 2 * MXU_DIM
FFN_TM = 1024
FFN_TF = 512


def _params(*sem):
    return pltpu.CompilerParams(dimension_semantics=sem, vmem_limit_bytes=VMEM_LIMIT)


def _rms_rows(x, gain):
    ms = jnp.mean(x * x, axis=-1, keepdims=True)
    return x * lax.rsqrt(ms + RMS_EPS) * gain


def _ffn_kernel(x_ref, g_ref, wg_ref, wu_ref, wd_ref, o_ref, h_ref):
    f = pl.program_id(1)
    tf = wd_ref.shape[0]

    @pl.when(f == 0)
    def _():
        x = x_ref[...]
        h_ref[...] = _rms_rows(x, g_ref[...]).astype(BF16)
        o_ref[...] = x

    h = h_ref[...]
    gate = jnp.dot(h, wg_ref[...], preferred_element_type=F32)
    up = jnp.dot(h, wu_ref[...], preferred_element_type=F32)
    a = (gate * jax.nn.sigmoid(gate)) * up * 0.5
    valid = D_FF - f * tf
    a = jnp.where(lax.broadcasted_iota(jnp.int32, (1, tf), 1) < valid, a, 0.0)
    wd = wd_ref[...]
    wd = jnp.where(lax.broadcasted_iota(jnp.int32, (tf, 1), 0) < valid, wd, jnp.zeros_like(wd))
    o_ref[...] += jnp.dot(a.astype(BF16), wd, preferred_element_type=F32)


def _ffn(x, gain, wg, wu, wd, layer):
    t, d = x.shape
    tm, tf = FFN_TM, FFN_TF
    return pl.pallas_call(
        _ffn_kernel,
        grid=(t // tm, pl.cdiv(D_FF, tf)),
        in_specs=[
            pl.BlockSpec((tm, d), lambda i, f: (i, 0)),
            pl.BlockSpec((1, d), lambda i, f: (0, 0)),
            pl.BlockSpec((None, d, tf), lambda i, f: (layer, 0, f)),
            pl.BlockSpec((None, d, tf), lambda i, f: (layer, 0, f)),
            pl.BlockSpec((None, tf, d), lambda i, f: (layer, f, 0)),
        ],
        out_specs=pl.BlockSpec((tm, d), lambda i, f: (i, 0)),
        out_shape=jax.ShapeDtypeStruct((t, d), F32),
        scratch_shapes=[pltpu.VMEM((tm, d), BF16)],
        compiler_params=_params("parallel", "arbitrary"),
        name="ffn",
    )(x, gain.reshape(1, d), wg, wu, wd)


def _qkv_kernel(x_ref, g_ref, w_ref, qg_ref, kg_ref, q_ref, k_ref, v_ref, h_ref):
    j = pl.program_id(1)

    @pl.when(j == 0)
    def _():
        h_ref[...] = _rms_rows(x_ref[...], g_ref[...]).astype(BF16)

    def head_norm(y, gain):
        cols = []
        for hh in range(y.shape[1] // HEAD_DIM):
            cols.append(_rms_rows(y[:, hh * HEAD_DIM:(hh + 1) * HEAD_DIM], gain))
        return jnp.concatenate(cols, axis=1)

    def project(out_ref, epilogue):
        h = h_ref[...]
        for c0 in range(0, w_ref.shape[1], SUB_N):
            y = jnp.dot(h, w_ref[:, c0:c0 + SUB_N], preferred_element_type=F32)
            out_ref[:, c0:c0 + SUB_N] = epilogue(y).astype(out_ref.dtype)

    @pl.when(j == 0)
    def _():
        project(q_ref, lambda y: head_norm(y, qg_ref[...]))

    @pl.when(j == 1)
    def _():
        project(k_ref, lambda y: head_norm(y, kg_ref[...]))

    @pl.when(j == 2)
    def _():
        project(v_ref, lambda y: y)


def _qkv_proj(x, gain, w, layer, q_gain, k_gain):
    t, d = x.shape
    tm = 512
    return pl.pallas_call(
        _qkv_kernel,
        grid=(t // tm, 3),
        in_specs=[
            pl.BlockSpec((tm, d), lambda i, j: (i, 0)),
            pl.BlockSpec((1, d), lambda i, j: (0, 0)),
            pl.BlockSpec((None, d, d), lambda i, j: (layer, 0, j)),
            pl.BlockSpec((1, HEAD_DIM), lambda i, j: (0, 0)),
            pl.BlockSpec((1, HEAD_DIM), lambda i, j: (0, 0)),
        ],
        out_specs=[
            pl.BlockSpec((tm, d), lambda i, j: (i, 0)),
            pl.BlockSpec((tm, d), lambda i, j: (i, 0)),
            pl.BlockSpec((tm, d), lambda i, j: (i, 0)),
        ],
        out_shape=[
            jax.ShapeDtypeStruct((t, d), F32),
            jax.ShapeDtypeStruct((t, d), BF16),
            jax.ShapeDtypeStruct((t, d), BF16),
        ],
        scratch_shapes=[pltpu.VMEM((tm, d), BF16)],
        compiler_params=_params("parallel", "arbitrary"),
        name="qkv_proj",
    )(x, gain.reshape(1, d), w, q_gain.reshape(1, HEAD_DIM), k_gain.reshape(1, HEAD_DIM))


def _attn_step(n_past, group, hh, i, slope_ref, k_ref, qa_ref, kt_ref, va_ref, o_ref):
    bs = MOBA_BLOCK
    c1 = HEAD_DIM ** -0.5 * LOG2E
    hcols = slice(hh * HEAD_DIM, (hh + 1) * HEAD_DIM)
    q_aug = qa_ref[hh, pl.ds(pl.multiple_of(i * bs, bs), bs), :]
    qb = q_aug[:, :HEAD_DIM]
    slope2 = slope_ref[hh] * LOG2E
    key_off = lax.broadcasted_iota(jnp.int32, (1, bs), 1)

    def col_bias(j):
        return slope2 * ((j - i) * bs + key_off).astype(F32)

    row = lax.broadcasted_iota(jnp.int32, (bs, bs), 0)
    col = lax.broadcasted_iota(jnp.int32, (bs, bs), 1)
    k_own = k_ref[pl.ds(pl.multiple_of(i * bs, bs), bs), hcols]
    s = lax.dot_general(qb, k_own, NT_DIMS, preferred_element_type=F32)
    t_own = jnp.where(row >= col, s * c1 + col_bias(i), -jnp.inf)
    va_own = va_ref[hh, pl.ds(pl.multiple_of(i * bs, bs), bs), :]

    def row_max(ts):
        tmax = functools.reduce(jnp.maximum, ts)
        return jnp.max(jnp.maximum(tmax[:, :LANES], tmax[:, LANES:]), axis=1, keepdims=True)

    m = acc = None
    for j0 in range(0, max(n_past, 1), group):
        j1 = min(j0 + group, n_past)
        ts = []
        if j1 > j0:
            s = jnp.dot(q_aug, kt_ref[hh, :, j0 * bs:j1 * bs], preferred_element_type=F32)
            ts = [s[:, (j - j0) * bs:(j - j0 + 1) * bs] * c1 + col_bias(j) for j in range(j0, j1)]
        if j0 == 0:
            m = row_max(ts + [t_own])
            acc = jnp.dot(jnp.exp2(t_own - m).astype(BF16), va_own, preferred_element_type=F32)
        else:
            m_new = jnp.maximum(m, row_max(ts))
            acc = jnp.exp2(m - m_new) * acc
            m = m_new
        if ts:
            pcat = jnp.concatenate([jnp.exp2(t - m).astype(BF16) for t in ts], axis=1)
            acc = acc + jnp.dot(pcat, va_ref[hh, j0 * bs:j1 * bs, :], preferred_element_type=F32)

    o_ref[:, hcols] = (acc[:, :HEAD_DIM] / acc[:, HEAD_DIM:]).astype(o_ref.dtype)


def _attn_prepare(hh, n_blocks, q_ref, k_ref, v_ref, qa_ref, kt_ref, va_ref, kmean_ref):
    bs = MOBA_BLOCK
    seq = q_ref.shape[0]
    hcols = slice(hh * HEAD_DIM, (hh + 1) * HEAD_DIM)
    va_ref[hh, :, :HEAD_DIM] = v_ref[:, hcols]
    va_ref[hh, :, HEAD_DIM:] = jnp.ones((seq, LANES), BF16)
    feat = lax.broadcasted_iota(jnp.int32, (LANES, bs), 0)
    for j in range(n_blocks):
        kj = k_ref[j * bs:(j + 1) * bs, hcols].astype(F32)
        kmean_ref[j:j + 1, :] = jnp.mean(kj, axis=0, keepdims=True)
        kt_ref[hh, :HEAD_DIM, j * bs:(j + 1) * bs] = kj.T.astype(BF16)
        kt_ref[hh, HEAD_DIM:, j * bs:(j + 1) * bs] = jnp.where(feat == j, 1.0, 0.0).astype(BF16)

    q32 = q_ref[:, hcols]
    gate = lax.dot_general(kmean_ref[...], q32, NT_DIMS, precision=lax.Precision.HIGHEST,
                           preferred_element_type=F32)
    blk = lax.broadcasted_iota(jnp.int32, gate.shape, 0)
    own = lax.broadcasted_iota(jnp.int32, gate.shape, 1) // bs
    rank = jnp.zeros(gate.shape, jnp.int32)
    for jp in range(n_blocks - 1):
        gj = gate[jp:jp + 1, :]
        beats = jnp.logical_or(gj > gate, jnp.logical_and(gj == gate, blk > jp))
        rank = rank + jnp.where(jnp.logical_and(beats, jp < own), 1, 0)
    selected = jnp.logical_and(blk < own, rank < MOBA_TOPK)
    bias_t = jnp.where(selected, 0.0, MASK_BIAS).astype(F32)
    pad = jnp.zeros((LANES - n_blocks, bs), F32)
    for ib in range(n_blocks):
        rows = slice(ib * bs, (ib + 1) * bs)
        chunk = jnp.concatenate([bias_t[:, rows], pad], axis=0)
        qa_ref[hh, rows, :HEAD_DIM] = q32[rows, :].astype(BF16)
        qa_ref[hh, rows, HEAD_DIM:] = chunk.T.astype(BF16)


def _attn_kernel(slope_ref, q_ref, k_ref, v_ref, o_ref, qa_ref, kt_ref, va_ref, kmean_ref, *, n_blocks,
                 group):
    i = pl.program_id(2)
    heads = q_ref.shape[1] // HEAD_DIM

    @pl.when(i == 0)
    def _():
        for hh in range(heads):
            _attn_prepare(hh, n_blocks, q_ref, k_ref, v_ref, qa_ref, kt_ref, va_ref, kmean_ref)

    n_groups = (i + group - 1) // group
    for c in range((n_blocks - 1 + group - 1) // group + 1):
        n_past = min(c * group, n_blocks - 1)

        @pl.when(n_groups == c)
        def _(n_past=n_past):
            for hh in range(heads):
                _attn_step(n_past, group, hh, i, slope_ref, k_ref, qa_ref, kt_ref, va_ref, o_ref)


def _moba_attention(q, k, v, batch, seq):
    t, d = q.shape
    bs = MOBA_BLOCK
    n_blocks = seq // bs
    slopes = np.asarray(2.0 ** (-8.0 * np.arange(1, N_HEADS + 1) / N_HEADS), dtype=np.float32)
    slopes = jnp.asarray(np.broadcast_to(slopes[:, None, None], (N_HEADS, 1, bs)))
    kern = functools.partial(_attn_kernel, n_blocks=n_blocks, group=ATTN_GROUP)
    hps = ATTN_HEADS_PER_STEP
    hw = hps * HEAD_DIM
    return pl.pallas_call(
        kern,
        grid=(batch, N_HEADS // hps, n_blocks),
        in_specs=[
            pl.BlockSpec((hps, 1, bs), lambda b, h, i: (h, 0, 0)),
            pl.BlockSpec((seq, hw), lambda b, h, i: (b, h)),
            pl.BlockSpec((seq, hw), lambda b, h, i: (b, h)),
            pl.BlockSpec((seq, hw), lambda b, h, i: (b, h)),
        ],
        out_specs=pl.BlockSpec((bs, hw), lambda b, h, i: (b * n_blocks + i, h)),
        out_shape=jax.ShapeDtypeStruct((t, d), BF16),
        scratch_shapes=[
            pltpu.VMEM((hps, seq, HEAD_DIM + LANES), BF16),
            pltpu.VMEM((hps, HEAD_DIM + LANES, seq), BF16),
            pltpu.VMEM((hps, seq, HEAD_DIM + LANES), BF16),
            pltpu.VMEM((n_blocks, HEAD_DIM), F32),
        ],
        compiler_params=_params("parallel", "parallel", "arbitrary"),
        name="moba_attn",
    )(slopes, q, k, v)


def _proj_res_kernel(x_ref, a_ref, w_ref, o_ref):
    o_ref[...] = x_ref[...] + jnp.dot(a_ref[...], w_ref[...], preferred_element_type=F32)


def _proj_residual(x, a, w, layer):
    t, n = x.shape
    kdim = a.shape[1]
    tm, tn = 512, 1024
    return pl.pallas_call(
        _proj_res_kernel,
        grid=(t // tm, n // tn),
        in_specs=[
            pl.BlockSpec((tm, tn), lambda i, j: (i, j)),
            pl.BlockSpec((tm, kdim), lambda i, j: (i, 0)),
            pl.BlockSpec((None, kdim, tn), lambda i, j: (layer, 0, j)),
        ],
        out_specs=pl.BlockSpec((tm, tn), lambda i, j: (i, j)),
        out_shape=jax.ShapeDtypeStruct((t, n), F32),
        compiler_params=_params("parallel", "parallel"),
        name="proj_residual",
    )(x, a, w)


def _pool_kernel(x_ref, halo_ref, g_ref, w_ref, sc_ref, o_ref, *, ts):
    i = pl.program_id(1)
    x = x_ref[...]
    gain = g_ref[...]
    h = _rms_rows(x, gain)
    h_halo = jnp.where(i == 0, 0.0, _rms_rows(halo_ref[...], gain))
    he = jnp.concatenate([h_halo, h], axis=0)
    pos = i * ts + lax.broadcasted_iota(jnp.int32, (ts, 1), 0)
    count = (pos + 1).astype(F32)
    cg = POOL_GROUP_DIM
    outs = []
    for g, win in enumerate(POOL_WINDOWS):
        a = he[:, g * cg:(g + 1) * cg]
        lead = 0
        width = 1
        while width < win:
            a = a[width:, :] + a[:-width, :]
            lead += width
            width *= 2
        start = POOL_HALO - lead
        pooled = a[start:start + ts, :] / jnp.minimum(count, float(win)) - h[:, g * cg:(g + 1) * cg]
        outs.append(jnp.dot(pooled.astype(BF16), w_ref[g], preferred_element_type=F32))
    y = jnp.concatenate(outs, axis=1)
    o_ref[...] = x + y * sc_ref[...]


def _pool_mixer(x, gain, w_group, scale, batch, seq):
    t, d = x.shape
    ts = 512
    per_seq = seq // ts
    halo_per_tile = ts // POOL_HALO
    kern = functools.partial(_pool_kernel, ts=ts)

    def halo_map(b, i):
        return (jnp.maximum((b * per_seq + i) * halo_per_tile - 1, 0), 0)

    return pl.pallas_call(
        kern,
        grid=(batch, per_seq),
        in_specs=[
            pl.BlockSpec((ts, d), lambda b, i: (b * per_seq + i, 0)),
            pl.BlockSpec((POOL_HALO, d), halo_map),
            pl.BlockSpec((1, d), lambda b, i: (0, 0)),
            pl.BlockSpec((N_POOL_GROUPS, POOL_GROUP_DIM, POOL_GROUP_DIM), lambda b, i: (0, 0, 0)),
            pl.BlockSpec((1, d), lambda b, i: (0, 0)),
        ],
        out_specs=pl.BlockSpec((ts, d), lambda b, i: (b * per_seq + i, 0)),
        out_shape=jax.ShapeDtypeStruct((t, d), F32),
        compiler_params=_params("parallel", "parallel"),
        name="pool_mixer",
    )(x, x, gain.reshape(1, d), w_group.astype(BF16), scale.reshape(1, d))


def _sgu_in_kernel(x_ref, g_ref, w_ref, z_ref, ssq_ref, h_ref, *, n_u):
    j = pl.program_id(1)

    @pl.when(j == 0)
    def _():
        h_ref[...] = _rms_rows(x_ref[...], g_ref[...]).astype(BF16)
        ssq_ref[...] = jnp.zeros_like(ssq_ref)

    def project(with_ssq):
        h = h_ref[...]
        part = None
        for c0 in range(0, w_ref.shape[1], SUB_N):
            y = jnp.dot(h, w_ref[:, c0:c0 + SUB_N], preferred_element_type=F32)
            z = 0.5 * y * (1.0 + lax.erf(y * np.float32(np.sqrt(0.5))))
            z_ref[:, c0:c0 + SUB_N] = z.astype(BF16)
            if with_ssq:
                zz = z * z
                for c in range(SUB_N // LANES):
                    sl = zz[:, c * LANES:(c + 1) * LANES]
                    part = sl if part is None else part + sl
        if with_ssq:
            ssq_ref[...] += part

    pl.when(j < n_u)(functools.partial(project, False))
    pl.when(j >= n_u)(functools.partial(project, True))


def _sgu_in(x, gain, w_in, layer):
    t, d = x.shape
    n = w_in.shape[2]
    tm, tn = 512, 2048
    n_u = (n // 2) // tn
    kern = functools.partial(_sgu_in_kernel, n_u=n_u)
    return pl.pallas_call(
        kern,
        grid=(t // tm, n // tn),
        in_specs=[
            pl.BlockSpec((tm, d), lambda i, j: (i, 0)),
            pl.BlockSpec((1, d), lambda i, j: (0, 0)),
            pl.BlockSpec((None, d, tn), lambda i, j: (layer, 0, j)),
        ],
        out_specs=[
            pl.BlockSpec((tm, tn), lambda i, j: (i, j)),
            pl.BlockSpec((tm, LANES), lambda i, j: (i, 0)),
        ],
        out_shape=[
            jax.ShapeDtypeStruct((t, n), BF16),
            jax.ShapeDtypeStruct((t, LANES), F32),
        ],
        scratch_shapes=[pltpu.VMEM((tm, d), BF16)],
        compiler_params=_params("parallel", "arbitrary"),
        name="sgu_in",
    )(x, gain.reshape(1, d), w_in)


def _sgu_out_kernel(x_ref, u_ref, v_ref, ssq_ref, vg_ref, ws_ref, bs_ref, wo_ref, o_ref, *, tm):
    g = pl.program_id(1)

    @pl.when(g == 0)
    def _():
        o_ref[...] = x_ref[...]

    inv = lax.rsqrt(jnp.sum(ssq_ref[...], axis=1, keepdims=True) / SGU_DIM + RMS_EPS)
    row = lax.broadcasted_iota(jnp.int32, (SGU_CHUNK, SGU_CHUNK), 0)
    col = lax.broadcasted_iota(jnp.int32, (SGU_CHUNK, SGU_CHUNK), 1)
    gd = SGU_GROUP_DIM
    acc = None
    for gg in range(ws_ref.shape[0]):
        gcols = slice(gg * gd, (gg + 1) * gd)
        vn = (v_ref[:, gcols].astype(F32) * inv * vg_ref[:, gcols]).astype(BF16)
        w_s = jnp.where(row >= col, ws_ref[gg], 0.0).astype(BF16)
        bias = bs_ref[gg][:, :1]
        parts = []
        for c in range(tm // SGU_CHUNK):
            vc = vn[c * SGU_CHUNK:(c + 1) * SGU_CHUNK, :]
            parts.append(jnp.dot(w_s, vc, preferred_element_type=F32) + bias)
        sv = jnp.concatenate(parts, axis=0)
        p = (u_ref[:, gcols].astype(F32) * sv).astype(BF16)
        y = jnp.dot(p, wo_ref[gcols, :], preferred_element_type=F32)
        acc = y if acc is None else acc + y
    o_ref[...] += acc


def _sgu_out(x, z, ssq, v_gain, w_spatial, b_spatial, w_out, layer):
    t, d = x.shape
    tm = 512
    gps = SGU_GROUPS_PER_STEP
    gw = gps * SGU_GROUP_DIM
    n_steps = SGU_GROUPS // gps
    b_rep = jnp.broadcast_to(b_spatial[:, :, None], (SGU_GROUPS, SGU_CHUNK, LANES))
    kern = functools.partial(_sgu_out_kernel, tm=tm)
    return pl.pallas_call(
        kern,
        grid=(t // tm, n_steps),
        in_specs=[
            pl.BlockSpec((tm, d), lambda i, g: (i, 0)),
            pl.BlockSpec((tm, gw), lambda i, g: (i, g)),
            pl.BlockSpec((tm, gw), lambda i, g: (i, n_steps + g)),
            pl.BlockSpec((tm, LANES), lambda i, g: (i, 0)),
            pl.BlockSpec((1, gw), lambda i, g: (0, g)),
            pl.BlockSpec((gps, SGU_CHUNK, SGU_CHUNK), lambda i, g: (g, 0, 0)),
            pl.BlockSpec((gps, SGU_CHUNK, LANES), lambda i, g: (g, 0, 0)),
            pl.BlockSpec((None, gw, d), lambda i, g: (layer, g, 0)),
        ],
        out_specs=pl.BlockSpec((tm, d), lambda i, g: (i, 0)),
        out_shape=jax.ShapeDtypeStruct((t, d), F32),
        compiler_params=_params("parallel", "arbitrary"),
        name="sgu_out",
    )(x, z, z, ssq, v_gain.reshape(1, SGU_DIM), w_spatial, b_rep, w_out)


def kernel(x, ffn1_norm, ffn1_w_gate, ffn1_w_up, ffn1_w_down, mix_norm, ffn2_norm, ffn2_w_gate, ffn2_w_up, ffn2_w_down, attn_w_qkv, attn_q_gain, attn_k_gain, attn_w_out, pool_w_group, pool_scale, sgu_w_in, sgu_v_gain, sgu_w_spatial, sgu_b_spatial, sgu_w_out):
    batch, seq, d = x.shape
    assert d == D_MODEL and seq % MOBA_BLOCK == 0 and seq % SGU_CHUNK == 0
    xt = x.reshape(batch * seq, d)
    ffn1 = [w.astype(BF16) for w in (ffn1_w_gate, ffn1_w_up, ffn1_w_down)]
    ffn2 = [w.astype(BF16) for w in (ffn2_w_gate, ffn2_w_up, ffn2_w_down)]
    w_qkv, w_attn_out = attn_w_qkv.astype(BF16), attn_w_out.astype(BF16)
    w_sgu_in, w_sgu_out = sgu_w_in.astype(BF16), sgu_w_out.astype(BF16)
    for i in range(DEPTH):
        xt = _ffn(xt, ffn1_norm[i], *ffn1, i)
        kind, j = i % N_MIXERS, i // N_MIXERS
        if kind == 0:
            q, k, v = _qkv_proj(xt, mix_norm[i], w_qkv, j, attn_q_gain[j], attn_k_gain[j])
            o = _moba_attention(q, k, v, batch, seq)
            xt = _proj_residual(xt, o, w_attn_out, j)
        elif kind == 1:
            xt = _pool_mixer(xt, mix_norm[i], pool_w_group[j], pool_scale[j], batch, seq)
        else:
            z, ssq = _sgu_in(xt, mix_norm[i], w_sgu_in, j)
            xt = _sgu_out(xt, z, ssq, sgu_v_gain[j], sgu_w_spatial[j], sgu_b_spatial[j], w_sgu_out, j)
        xt = _ffn(xt, ffn2_norm[i], *ffn2, i)
    return xt.reshape(batch, seq, d)
```

```python
import functools

import jax
import jax.numpy as jnp
import numpy as np
from jax import lax
from jax.experimental import pallas as pl
from jax.experimental.pallas import tpu as pltpu

D_MODEL = 2048
DEPTH = 4
N_MIXERS = 3
RMS_EPS = 1e-6
D_FF = 5504
N_HEADS = 16
HEAD_DIM = D_MODEL // N_HEADS
MOBA_BLOCK = 256
MOBA_TOPK = 3
POOL_WINDOWS = (2, 4, 8, 16)
N_POOL_GROUPS = len(POOL_WINDOWS)
POOL_GROUP_DIM = D_MODEL // N_POOL_GROUPS
POOL_HALO = max(POOL_WINDOWS)
SGU_DIM = 3 * D_MODEL
SGU_CHUNK = 128
SGU_GROUPS = 8
SGU_GROUP_DIM = SGU_DIM // SGU_GROUPS

LANES = 128
MXU_DIM = 256
VMEM_LIMIT = 56 * 1024 * 1024

F32 = jnp.float32
BF16 = jnp.bfloat16
MASK_BIAS = -1e9
LOG2E = float(np.log2(np.e))
ATTN_GROUP = 4
ATTN_HEADS_PER_STEP = 2
ATTN_QBLOCKS_PER_STEP = 2
SGU_GROUPS_PER_STEP = 2
NT_DIMS = (((1,), (1,)), ((), ()))


SUB_N = 2 * MXU_DIM
FFN_TM = 1024
FFN_TF = 512


def _params(*sem):
    return pltpu.CompilerParams(dimension_semantics=sem, vmem_limit_bytes=VMEM_LIMIT)


def _rms_rows(x, gain):
    ms = jnp.mean(x * x, axis=-1, keepdims=True)
    return x * lax.rsqrt(ms + RMS_EPS) * gain


def _ffn_kernel(x_ref, g_ref, wg_ref, wu_ref, wd_ref, o_ref, h_ref):
    f = pl.program_id(1)
    tf = wd_ref.shape[0]

    @pl.when(f == 0)
    def _():
        x = x_ref[...]
        h_ref[...] = _rms_rows(x, g_ref[...]).astype(BF16)
        o_ref[...] = x

    h = h_ref[...]
    gate = jnp.dot(h, wg_ref[...], preferred_element_type=F32)
    up = jnp.dot(h, wu_ref[...], preferred_element_type=F32)
    a = (gate * jax.nn.sigmoid(gate)) * up * 0.5
    valid = D_FF - f * tf
    a = jnp.where(lax.broadcasted_iota(jnp.int32, (1, tf), 1) < valid, a, 0.0)
    wd = wd_ref[...]
    wd = jnp.where(lax.broadcasted_iota(jnp.int32, (tf, 1), 0) < valid, wd, jnp.zeros_like(wd))
    o_ref[...] += jnp.dot(a.astype(BF16), wd, preferred_element_type=F32)


def _ffn(x, gain, wg, wu, wd, layer):
    t, d = x.shape
    tm, tf = FFN_TM, FFN_TF
    return pl.pallas_call(
        _ffn_kernel,
        grid=(t // tm, pl.cdiv(D_FF, tf)),
        in_specs=[
            pl.BlockSpec((tm, d), lambda i, f: (i, 0)),
            pl.BlockSpec((1, d), lambda i, f: (0, 0)),
            pl.BlockSpec((None, d, tf), lambda i, f: (layer, 0, f)),
            pl.BlockSpec((None, d, tf), lambda i, f: (layer, 0, f)),
            pl.BlockSpec((None, tf, d), lambda i, f: (layer, f, 0)),
        ],
        out_specs=pl.BlockSpec((tm, d), lambda i, f: (i, 0)),
        out_shape=jax.ShapeDtypeStruct((t, d), F32),
        scratch_shapes=[pltpu.VMEM((tm, d), BF16)],
        compiler_params=_params("parallel", "arbitrary"),
        name="ffn",
    )(x, gain.reshape(1, d), wg, wu, wd)


def _qkv_kernel(x_ref, g_ref, w_ref, qg_ref, kg_ref, q_ref, k_ref, v_ref, h_ref):
    j = pl.program_id(1)

    @pl.when(j == 0)
    def _():
        h_ref[...] = _rms_rows(x_ref[...], g_ref[...]).astype(BF16)

    def head_norm(y, gain):
        cols = []
        for hh in range(y.shape[1] // HEAD_DIM):
            cols.append(_rms_rows(y[:, hh * HEAD_DIM:(hh + 1) * HEAD_DIM], gain))
        return jnp.concatenate(cols, axis=1)

    def project(out_ref, epilogue):
        h = h_ref[...]
        for c0 in range(0, w_ref.shape[1], SUB_N):
            y = jnp.dot(h, w_ref[:, c0:c0 + SUB_N], preferred_element_type=F32)
            out_ref[:, c0:c0 + SUB_N] = epilogue(y).astype(out_ref.dtype)

    @pl.when(j == 0)
    def _():
        project(q_ref, lambda y: head_norm(y, qg_ref[...]))

    @pl.when(j == 1)
    def _():
        project(k_ref, lambda y: head_norm(y, kg_ref[...]))

    @pl.when(j == 2)
    def _():
        project(v_ref, lambda y: y)


def _qkv_proj(x, gain, w, layer, q_gain, k_gain):
    t, d = x.shape
    tm = 512
    return pl.pallas_call(
        _qkv_kernel,
        grid=(t // tm, 3),
        in_specs=[
            pl.BlockSpec((tm, d), lambda i, j: (i, 0)),
            pl.BlockSpec((1, d), lambda i, j: (0, 0)),
            pl.BlockSpec((None, d, d), lambda i, j: (layer, 0, j)),
            pl.BlockSpec((1, HEAD_DIM), lambda i, j: (0, 0)),
            pl.BlockSpec((1, HEAD_DIM), lambda i, j: (0, 0)),
        ],
        out_specs=[
            pl.BlockSpec((tm, d), lambda i, j: (i, 0)),
            pl.BlockSpec((tm, d), lambda i, j: (i, 0)),
            pl.BlockSpec((tm, d), lambda i, j: (i, 0)),
        ],
        out_shape=[
            jax.ShapeDtypeStruct((t, d), F32),
            jax.ShapeDtypeStruct((t, d), BF16),
            jax.ShapeDtypeStruct((t, d), BF16),
        ],
        scratch_shapes=[pltpu.VMEM((tm, d), BF16)],
        compiler_params=_params("parallel", "arbitrary"),
        name="qkv_proj",
    )(x, gain.reshape(1, d), w, q_gain.reshape(1, HEAD_DIM), k_gain.reshape(1, HEAD_DIM))


def _attn_step(i, group, hh, out_rows, slope_ref, qa_ref, kt_ref, va_ref, o_ref):
    bs = MOBA_BLOCK
    c1 = HEAD_DIM ** -0.5 * LOG2E
    hcols = slice(hh * HEAD_DIM, (hh + 1) * HEAD_DIM)
    own = slice(i * bs, (i + 1) * bs)
    q_aug = qa_ref[hh, own, :]
    slope2 = slope_ref[hh] * LOG2E
    key_off = lax.broadcasted_iota(jnp.int32, (1, bs), 1)

    def col_bias(j):
        return slope2 * ((j - i) * bs + key_off).astype(F32)

    row = lax.broadcasted_iota(jnp.int32, (bs, bs), 0)
    col = lax.broadcasted_iota(jnp.int32, (bs, bs), 1)
    s = jnp.dot(q_aug[:, :HEAD_DIM], kt_ref[hh, :HEAD_DIM, own], preferred_element_type=F32)
    t_own = jnp.where(row >= col, s * c1 + col_bias(i), -jnp.inf)
    va_own = va_ref[hh, own, :]
    n_past = i

    def row_max(ts):
        tmax = functools.reduce(jnp.maximum, ts)
        return jnp.max(jnp.maximum(tmax[:, :LANES], tmax[:, LANES:]), axis=1, keepdims=True)

    m = acc = None
    n_groups = max(1, (n_past + group // 2) // group)
    bounds = [n_past * g // n_groups for g in range(n_groups + 1)]
    for j0, j1 in zip(bounds[:-1], bounds[1:]):
        ts = []
        if j1 > j0:
            s = jnp.dot(q_aug, kt_ref[hh, :, j0 * bs:j1 * bs], preferred_element_type=F32)
            ts = [s[:, (j - j0) * bs:(j - j0 + 1) * bs] * c1 + col_bias(j) for j in range(j0, j1)]
        if j0 == 0:
            m = row_max(ts + [t_own])
            acc = jnp.dot(jnp.exp2(t_own - m).astype(BF16), va_own, preferred_element_type=F32)
        else:
            m_new = jnp.maximum(m, row_max(ts))
            acc = jnp.exp2(m - m_new) * acc
            m = m_new
        if ts:
            pcat = jnp.concatenate([jnp.exp2(t - m).astype(BF16) for t in ts], axis=1)
            acc = acc + jnp.dot(pcat, va_ref[hh, j0 * bs:j1 * bs, :], preferred_element_type=F32)

    o_ref[out_rows, hcols] = (acc[:, :HEAD_DIM] / acc[:, HEAD_DIM:]).astype(o_ref.dtype)


def _attn_prepare(hh, n_blocks, q_ref, k_ref, v_ref, qa_ref, kt_ref, va_ref, kmean_ref):
    bs = MOBA_BLOCK
    seq = q_ref.shape[0]
    hcols = slice(hh * HEAD_DIM, (hh + 1) * HEAD_DIM)
    va_ref[hh, :, :HEAD_DIM] = v_ref[:, hcols]
    va_ref[hh, :, HEAD_DIM:] = jnp.ones((seq, LANES), BF16)
    feat = lax.broadcasted_iota(jnp.int32, (LANES, bs), 0)
    for j in range(n_blocks):
        kj = k_ref[j * bs:(j + 1) * bs, hcols].astype(F32)
        kmean_ref[j:j + 1, :] = jnp.mean(kj, axis=0, keepdims=True)
        kt_ref[hh, :HEAD_DIM, j * bs:(j + 1) * bs] = kj.T.astype(BF16)
        kt_ref[hh, HEAD_DIM:, j * bs:(j + 1) * bs] = jnp.where(feat == j, 1.0, 0.0).astype(BF16)

    q32 = q_ref[:, hcols]
    q_hi = q32.astype(BF16)
    q_lo = (q32 - q_hi.astype(F32)).astype(BF16)
    km = kmean_ref[...]
    km_hi = km.astype(BF16)
    km_lo = (km - km_hi.astype(F32)).astype(BF16)

    def dot_nt(a, b):
        return lax.dot_general(a, b, NT_DIMS, preferred_element_type=F32)

    gate = dot_nt(km_hi, q_hi) + (dot_nt(km_hi, q_lo) + dot_nt(km_lo, q_hi))
    blk = lax.broadcasted_iota(jnp.int32, gate.shape, 0)
    own = lax.broadcasted_iota(jnp.int32, gate.shape, 1) // bs
    rank = jnp.zeros(gate.shape, jnp.int32)
    for jp in range(n_blocks - 1):
        gj = gate[jp:jp + 1, :]
        beats = jnp.logical_or(gj > gate, jnp.logical_and(gj == gate, blk > jp))
        rank = rank + jnp.where(jnp.logical_and(beats, jp < own), 1, 0)
    selected = jnp.logical_and(blk < own, rank < MOBA_TOPK)
    bias_t = jnp.where(selected, 0.0, MASK_BIAS).astype(F32)
    pad = jnp.zeros((LANES - n_blocks, bs), F32)
    for ib in range(n_blocks):
        rows = slice(ib * bs, (ib + 1) * bs)
        chunk = jnp.concatenate([bias_t[:, rows], pad], axis=0)
        qa_ref[hh, rows, :HEAD_DIM] = q_hi[rows, :]
        qa_ref[hh, rows, HEAD_DIM:] = chunk.T.astype(BF16)


def _attn_kernel(slope_ref, q_ref, k_ref, v_ref, o_ref, qa_ref, kt_ref, va_ref, kmean_ref, *, n_blocks,
                 group):
    bs = MOBA_BLOCK
    step = pl.program_id(2)
    heads = q_ref.shape[1] // HEAD_DIM
    q_blocks = o_ref.shape[0] // bs

    @pl.when(step == 0)
    def _():
        for hh in range(heads):
            _attn_prepare(hh, n_blocks, q_ref, k_ref, v_ref, qa_ref, kt_ref, va_ref, kmean_ref)

    for c in range(n_blocks // q_blocks):
        @pl.when(step == c)
        def _(c=c):
            for sub in range(q_blocks):
                for hh in range(heads):
                    _attn_step(c * q_blocks + sub, group, hh, slice(sub * bs, (sub + 1) * bs), slope_ref,
                               qa_ref, kt_ref, va_ref, o_ref)


def _moba_attention(q, k, v, batch, seq):
    t, d = q.shape
    bs = MOBA_BLOCK
    n_blocks = seq // bs
    slopes = np.asarray(2.0 ** (-8.0 * np.arange(1, N_HEADS + 1) / N_HEADS), dtype=np.float32)
    slopes = jnp.asarray(np.broadcast_to(slopes[:, None, None], (N_HEADS, 1, bs)))
    kern = functools.partial(_attn_kernel, n_blocks=n_blocks, group=ATTN_GROUP)
    hps = ATTN_HEADS_PER_STEP
    hw = hps * HEAD_DIM
    qbs = ATTN_QBLOCKS_PER_STEP
    n_steps = n_blocks // qbs
    return pl.pallas_call(
        kern,
        grid=(batch, N_HEADS // hps, n_steps),
        in_specs=[
            pl.BlockSpec((hps, 1, bs), lambda b, h, i: (h, 0, 0)),
            pl.BlockSpec((seq, hw), lambda b, h, i: (b, h)),
            pl.BlockSpec((seq, hw), lambda b, h, i: (b, h)),
            pl.BlockSpec((seq, hw), lambda b, h, i: (b, h)),
        ],
        out_specs=pl.BlockSpec((qbs * bs, hw), lambda b, h, i: (b * n_steps + i, h)),
        out_shape=jax.ShapeDtypeStruct((t, d), BF16),
        scratch_shapes=[
            pltpu.VMEM((hps, seq, HEAD_DIM + LANES), BF16),
            pltpu.VMEM((hps, HEAD_DIM + LANES, seq), BF16),
            pltpu.VMEM((hps, seq, HEAD_DIM + LANES), BF16),
            pltpu.VMEM((n_blocks, HEAD_DIM), F32),
        ],
        compiler_params=_params("parallel", "parallel", "arbitrary"),
        name="moba_attn",
    )(slopes, q, k, v)


def _proj_res_kernel(x_ref, a_ref, w_ref, o_ref):
    a = a_ref[...]
    for c0 in range(0, w_ref.shape[1], SUB_N):
        cols = slice(c0, c0 + SUB_N)
        o_ref[:, cols] = x_ref[:, cols] + jnp.dot(a, w_ref[:, cols], preferred_element_type=F32)


def _proj_residual(x, a, w, layer):
    t, n = x.shape
    kdim = a.shape[1]
    tm = 512
    return pl.pallas_call(
        _proj_res_kernel,
        grid=(t // tm,),
        in_specs=[
            pl.BlockSpec((tm, n), lambda i: (i, 0)),
            pl.BlockSpec((tm, kdim), lambda i: (i, 0)),
            pl.BlockSpec((None, kdim, n), lambda i: (layer, 0, 0)),
        ],
        out_specs=pl.BlockSpec((tm, n), lambda i: (i, 0)),
        out_shape=jax.ShapeDtypeStruct((t, n), F32),
        compiler_params=_params("parallel"),
        name="proj_residual",
    )(x, a, w)


def _pool_kernel(x_ref, halo_ref, g_ref, w_ref, sc_ref, o_ref, *, ts):
    i = pl.program_id(1)
    x = x_ref[...]
    gain = g_ref[...]
    h = _rms_rows(x, gain)
    h_halo = jnp.where(i == 0, 0.0, _rms_rows(halo_ref[...], gain))
    he = jnp.concatenate([h_halo, h], axis=0)
    pos = i * ts + lax.broadcasted_iota(jnp.int32, (ts, 1), 0)
    count = (pos + 1).astype(F32)
    cg = POOL_GROUP_DIM
    outs = []
    for g, win in enumerate(POOL_WINDOWS):
        a = he[:, g * cg:(g + 1) * cg]
        lead = 0
        width = 1
        while width < win:
            a = a[width:, :] + a[:-width, :]
            lead += width
            width *= 2
        start = POOL_HALO - lead
        pooled = a[start:start + ts, :] / jnp.minimum(count, float(win)) - h[:, g * cg:(g + 1) * cg]
        outs.append(jnp.dot(pooled.astype(BF16), w_ref[g], preferred_element_type=F32))
    y = jnp.concatenate(outs, axis=1)
    o_ref[...] = x + y * sc_ref[...]


def _pool_mixer(x, gain, w_group, scale, batch, seq):
    t, d = x.shape
    ts = 512
    per_seq = seq // ts
    halo_per_tile = ts // POOL_HALO
    kern = functools.partial(_pool_kernel, ts=ts)

    def halo_map(b, i):
        return (jnp.maximum((b * per_seq + i) * halo_per_tile - 1, 0), 0)

    return pl.pallas_call(
        kern,
        grid=(batch, per_seq),
        in_specs=[
            pl.BlockSpec((ts, d), lambda b, i: (b * per_seq + i, 0)),
            pl.BlockSpec((POOL_HALO, d), halo_map),
            pl.BlockSpec((1, d), lambda b, i: (0, 0)),
            pl.BlockSpec((N_POOL_GROUPS, POOL_GROUP_DIM, POOL_GROUP_DIM), lambda b, i: (0, 0, 0)),
            pl.BlockSpec((1, d), lambda b, i: (0, 0)),
        ],
        out_specs=pl.BlockSpec((ts, d), lambda b, i: (b * per_seq + i, 0)),
        out_shape=jax.ShapeDtypeStruct((t, d), F32),
        compiler_params=_params("parallel", "parallel"),
        name="pool_mixer",
    )(x, x, gain.reshape(1, d), w_group.astype(BF16), scale.reshape(1, d))


def _sgu_in_kernel(x_ref, g_ref, w_ref, z_ref, ssq_ref, h_ref, *, n_u):
    j = pl.program_id(1)

    @pl.when(j == 0)
    def _():
        h_ref[...] = _rms_rows(x_ref[...], g_ref[...]).astype(BF16)
        ssq_ref[...] = jnp.zeros_like(ssq_ref)

    def project(with_ssq):
        h = h_ref[...]
        part = None
        for c0 in range(0, w_ref.shape[1], SUB_N):
            y = jnp.dot(h, w_ref[:, c0:c0 + SUB_N], preferred_element_type=F32)
            z = 0.5 * y * (1.0 + lax.erf(y * np.float32(np.sqrt(0.5))))
            z_ref[:, c0:c0 + SUB_N] = z.astype(BF16)
            if with_ssq:
                zz = z * z
                for c in range(SUB_N // LANES):
                    sl = zz[:, c * LANES:(c + 1) * LANES]
                    part = sl if part is None else part + sl
        if with_ssq:
            ssq_ref[...] += part

    pl.when(j < n_u)(functools.partial(project, False))
    pl.when(j >= n_u)(functools.partial(project, True))


def _sgu_in(x, gain, w_in, layer):
    t, d = x.shape
    n = w_in.shape[2]
    tm, tn = 512, 2048
    n_u = (n // 2) // tn
    kern = functools.partial(_sgu_in_kernel, n_u=n_u)
    return pl.pallas_call(
        kern,
        grid=(t // tm, n // tn),
        in_specs=[
            pl.BlockSpec((tm, d), lambda i, j: (i, 0)),
            pl.BlockSpec((1, d), lambda i, j: (0, 0)),
            pl.BlockSpec((None, d, tn), lambda i, j: (layer, 0, j)),
        ],
        out_specs=[
            pl.BlockSpec((tm, tn), lambda i, j: (i, j)),
            pl.BlockSpec((tm, LANES), lambda i, j: (i, 0)),
        ],
        out_shape=[
            jax.ShapeDtypeStruct((t, n), BF16),
            jax.ShapeDtypeStruct((t, LANES), F32),
        ],
        scratch_shapes=[pltpu.VMEM((tm, d), BF16)],
        compiler_params=_params("parallel", "arbitrary"),
        name="sgu_in",
    )(x, gain.reshape(1, d), w_in)


def _sgu_out_kernel(x_ref, u_ref, v_ref, ssq_ref, vg_ref, ws_ref, bs_ref, wo_ref, o_ref, *, tm):
    g = pl.program_id(1)

    @pl.when(g == 0)
    def _():
        o_ref[...] = x_ref[...]

    inv = lax.rsqrt(jnp.sum(ssq_ref[...], axis=1, keepdims=True) / SGU_DIM + RMS_EPS)
    row = lax.broadcasted_iota(jnp.int32, (SGU_CHUNK, SGU_CHUNK), 0)
    col = lax.broadcasted_iota(jnp.int32, (SGU_CHUNK, SGU_CHUNK), 1)
    gd = SGU_GROUP_DIM
    acc = None
    for gg in range(ws_ref.shape[0]):
        gcols = slice(gg * gd, (gg + 1) * gd)
        vn = (v_ref[:, gcols].astype(F32) * inv * vg_ref[:, gcols]).astype(BF16)
        w_s = jnp.where(row >= col, ws_ref[gg], 0.0).astype(BF16)
        bias = bs_ref[gg][:, :1]
        parts = []
        for c in range(tm // SGU_CHUNK):
            vc = vn[c * SGU_CHUNK:(c + 1) * SGU_CHUNK, :]
            parts.append(jnp.dot(w_s, vc, preferred_element_type=F32) + bias)
        sv = jnp.concatenate(parts, axis=0)
        p = (u_ref[:, gcols].astype(F32) * sv).astype(BF16)
        y = jnp.dot(p, wo_ref[gcols, :], preferred_element_type=F32)
        acc = y if acc is None else acc + y
    o_ref[...] += acc


def _sgu_out(x, z, ssq, v_gain, w_spatial, b_spatial, w_out, layer):
    t, d = x.shape
    tm = 512
    gps = SGU_GROUPS_PER_STEP
    gw = gps * SGU_GROUP_DIM
    n_steps = SGU_GROUPS // gps
    b_rep = jnp.broadcast_to(b_spatial[:, :, None], (SGU_GROUPS, SGU_CHUNK, LANES))
    kern = functools.partial(_sgu_out_kernel, tm=tm)
    return pl.pallas_call(
        kern,
        grid=(t // tm, n_steps),
        in_specs=[
            pl.BlockSpec((tm, d), lambda i, g: (i, 0)),
            pl.BlockSpec((tm, gw), lambda i, g: (i, g)),
            pl.BlockSpec((tm, gw), lambda i, g: (i, n_steps + g)),
            pl.BlockSpec((tm, LANES), lambda i, g: (i, 0)),
            pl.BlockSpec((1, gw), lambda i, g: (0, g)),
            pl.BlockSpec((gps, SGU_CHUNK, SGU_CHUNK), lambda i, g: (g, 0, 0)),
            pl.BlockSpec((gps, SGU_CHUNK, LANES), lambda i, g: (g, 0, 0)),
            pl.BlockSpec((None, gw, d), lambda i, g: (layer, g, 0)),
        ],
        out_specs=pl.BlockSpec((tm, d), lambda i, g: (i, 0)),
        out_shape=jax.ShapeDtypeStruct((t, d), F32),
        compiler_params=_params("parallel", "arbitrary"),
        name="sgu_out",
    )(x, z, z, ssq, v_gain.reshape(1, SGU_DIM), w_spatial, b_rep, w_out)


def kernel(x, ffn1_norm, ffn1_w_gate, ffn1_w_up, ffn1_w_down, mix_norm, ffn2_norm, ffn2_w_gate, ffn2_w_up, ffn2_w_down, attn_w_qkv, attn_q_gain, attn_k_gain, attn_w_out, pool_w_group, pool_scale, sgu_w_in, sgu_v_gain, sgu_w_spatial, sgu_b_spatial, sgu_w_out):
    batch, seq, d = x.shape
    assert d == D_MODEL and seq % MOBA_BLOCK == 0 and seq % SGU_CHUNK == 0
    xt = x.reshape(batch * seq, d)
    ffn1 = [w.astype(BF16) for w in (ffn1_w_gate, ffn1_w_up, ffn1_w_down)]
    ffn2 = [w.astype(BF16) for w in (ffn2_w_gate, ffn2_w_up, ffn2_w_down)]
    w_qkv, w_attn_out = attn_w_qkv.astype(BF16), attn_w_out.astype(BF16)
    w_sgu_in, w_sgu_out = sgu_w_in.astype(BF16), sgu_w_out.astype(BF16)
    for i in range(DEPTH):
        xt = _ffn(xt, ffn1_norm[i], *ffn1, i)
        kind, j = i % N_MIXERS, i // N_MIXERS
        if kind == 0:
            q, k, v = _qkv_proj(xt, mix_norm[i], w_qkv, j, attn_q_gain[j], attn_k_gain[j])
            o = _moba_attention(q, k, v, batch, seq)
            xt = _proj_residual(xt, o, w_attn_out, j)
        elif kind == 1:
            xt = _pool_mixer(xt, mix_norm[i], pool_w_group[j], pool_scale[j], batch, seq)
        else:
            z, ssq = _sgu_in(xt, mix_norm[i], w_sgu_in, j)
            xt = _sgu_out(xt, z, ssq, sgu_v_gain[j], sgu_w_spatial[j], sgu_b_spatial[j], w_sgu_out, j)
        xt = _ffn(xt, ffn2_norm[i], *ffn2, i)
    return xt.reshape(batch, seq, d)
```

```python
import functools

import jax
import jax.numpy as jnp
import numpy as np
from jax import lax
from jax.experimental import pallas as pl
from jax.experimental.pallas import tpu as pltpu

D_MODEL = 2048
DEPTH = 4
N_MIXERS = 3
RMS_EPS = 1e-6
D_FF = 5504
N_HEADS = 16
HEAD_DIM = D_MODEL // N_HEADS
MOBA_BLOCK = 256
MOBA_TOPK = 3
POOL_WINDOWS = (2, 4, 8, 16)
N_POOL_GROUPS = len(POOL_WINDOWS)
POOL_GROUP_DIM = D_MODEL // N_POOL_GROUPS
POOL_HALO = max(POOL_WINDOWS)
SGU_DIM = 3 * D_MODEL
SGU_CHUNK = 128
SGU_GROUPS = 8
SGU_GROUP_DIM = SGU_DIM // SGU_GROUPS

LANES = 128
MXU_DIM = 256
VMEM_LIMIT = 58 * 1024 * 1024

F32 = jnp.float32
BF16 = jnp.bfloat16
MASK_BIAS = -1e9
LOG2E = float(np.log2(np.e))
ATTN_GROUP = 4
ATTN_HEADS_PER_STEP = 2
ATTN_QBLOCKS_PER_STEP = 2
SGU_GROUPS_PER_STEP = 2
NT_DIMS = (((1,), (1,)), ((), ()))


SUB_N = 2 * MXU_DIM
FFN_TM = 1024
FFN_TF = 512


def _params(*sem):
    return pltpu.CompilerParams(dimension_semantics=sem, vmem_limit_bytes=VMEM_LIMIT)


def _rms_rows(x, gain):
    ms = jnp.mean(x * x, axis=-1, keepdims=True)
    return x * lax.rsqrt(ms + RMS_EPS) * gain


def _ffn_kernel(*refs, convert_next):
    if convert_next:
        x_ref, g_ref, wg_ref, wu_ref, wd_ref, *next_f32, o_ref, og_ref, ou_ref, od_ref, h_ref = refs
        for src, dst in zip(next_f32, (og_ref, ou_ref, od_ref)):
            dst[...] = src[...].astype(BF16)
    else:
        x_ref, g_ref, wg_ref, wu_ref, wd_ref, o_ref, h_ref = refs
    f = pl.program_id(1)
    tf = wd_ref.shape[0]

    def step(first):
        if first:
            x = x_ref[...]
            h = _rms_rows(x, g_ref[...]).astype(BF16)
            h_ref[...] = h
        else:
            h = h_ref[...]
        gate = jnp.dot(h, wg_ref[...], preferred_element_type=F32)
        up = jnp.dot(h, wu_ref[...], preferred_element_type=F32)
        a = (gate * jax.nn.sigmoid(gate)) * up * 0.5
        valid = D_FF - f * tf
        a = jnp.where(lax.broadcasted_iota(jnp.int32, (1, tf), 1) < valid, a, 0.0)
        wd = wd_ref[...]
        wd = jnp.where(lax.broadcasted_iota(jnp.int32, (tf, 1), 0) < valid, wd, jnp.zeros_like(wd))
        y = jnp.dot(a.astype(BF16), wd, preferred_element_type=F32)
        if first:
            o_ref[...] = x + y
        else:
            o_ref[...] += y

    pl.when(f == 0)(functools.partial(step, True))
    pl.when(f > 0)(functools.partial(step, False))


def _ffn(x, gain, weights, next_f32=None, next_layer=None):
    t, d = x.shape
    tm, tf = FFN_TM, FFN_TF
    n_tiles = t // tm
    in_specs = [
        pl.BlockSpec((tm, d), lambda i, f: (i, 0)),
        pl.BlockSpec((1, d), lambda i, f: (0, 0)),
        pl.BlockSpec((d, tf), lambda i, f: (0, f)),
        pl.BlockSpec((d, tf), lambda i, f: (0, f)),
        pl.BlockSpec((tf, d), lambda i, f: (f, 0)),
    ]
    out_specs = [pl.BlockSpec((tm, d), lambda i, f: (i, 0))]
    out_shape = [jax.ShapeDtypeStruct((t, d), F32)]
    operands = [x, gain.reshape(1, d), *weights]
    if next_f32 is not None:
        rows = d // n_tiles
        in_specs += [
            pl.BlockSpec((None, rows, tf), lambda i, f: (next_layer, i, f)),
            pl.BlockSpec((None, rows, tf), lambda i, f: (next_layer, i, f)),
            pl.BlockSpec((None, tf, rows), lambda i, f: (next_layer, f, i)),
        ]
        out_specs += [
            pl.BlockSpec((rows, tf), lambda i, f: (i, f)),
            pl.BlockSpec((rows, tf), lambda i, f: (i, f)),
            pl.BlockSpec((tf, rows), lambda i, f: (f, i)),
        ]
        out_shape += [jax.ShapeDtypeStruct(w.shape[1:], BF16) for w in next_f32]
        operands += list(next_f32)
    outs = pl.pallas_call(
        functools.partial(_ffn_kernel, convert_next=next_f32 is not None),
        grid=(n_tiles, pl.cdiv(D_FF, tf)),
        in_specs=in_specs,
        out_specs=out_specs,
        out_shape=out_shape,
        scratch_shapes=[pltpu.VMEM((tm, d), BF16)],
        compiler_params=_params("parallel", "arbitrary"),
        name="ffn",
    )(*operands)
    return outs[0], tuple(outs[1:])


def _qkv_kernel(x_ref, g_ref, w_ref, qg_ref, kg_ref, q_ref, k_ref, v_ref, h_ref):
    j = pl.program_id(1)

    def head_norm(y, gain):
        cols = []
        for hh in range(y.shape[1] // HEAD_DIM):
            cols.append(_rms_rows(y[:, hh * HEAD_DIM:(hh + 1) * HEAD_DIM], gain))
        return jnp.concatenate(cols, axis=1)

    def project(out_ref, epilogue, first=False):
        if first:
            h = _rms_rows(x_ref[...], g_ref[...]).astype(BF16)
            h_ref[...] = h
        else:
            h = h_ref[...]
        for c0 in range(0, w_ref.shape[1], SUB_N):
            y = jnp.dot(h, w_ref[:, c0:c0 + SUB_N], preferred_element_type=F32)
            out_ref[:, c0:c0 + SUB_N] = epilogue(y).astype(out_ref.dtype)

    @pl.when(j == 0)
    def _():
        project(q_ref, lambda y: head_norm(y, qg_ref[...]), first=True)

    @pl.when(j == 1)
    def _():
        project(k_ref, lambda y: head_norm(y, kg_ref[...]))

    @pl.when(j == 2)
    def _():
        project(v_ref, lambda y: y)


def _qkv_proj(x, gain, w, layer, q_gain, k_gain):
    t, d = x.shape
    tm = 512
    return pl.pallas_call(
        _qkv_kernel,
        grid=(t // tm, 3),
        in_specs=[
            pl.BlockSpec((tm, d), lambda i, j: (i, 0)),
            pl.BlockSpec((1, d), lambda i, j: (0, 0)),
            pl.BlockSpec((None, d, d), lambda i, j: (layer, 0, j)),
            pl.BlockSpec((1, HEAD_DIM), lambda i, j: (0, 0)),
            pl.BlockSpec((1, HEAD_DIM), lambda i, j: (0, 0)),
        ],
        out_specs=[
            pl.BlockSpec((tm, d), lambda i, j: (i, 0)),
            pl.BlockSpec((tm, d), lambda i, j: (i, 0)),
            pl.BlockSpec((tm, d), lambda i, j: (i, 0)),
        ],
        out_shape=[
            jax.ShapeDtypeStruct((t, d), F32),
            jax.ShapeDtypeStruct((t, d), BF16),
            jax.ShapeDtypeStruct((t, d), BF16),
        ],
        scratch_shapes=[pltpu.VMEM((tm, d), BF16)],
        compiler_params=_params("parallel", "arbitrary"),
        name="qkv_proj",
    )(x, gain.reshape(1, d), w, q_gain.reshape(1, HEAD_DIM), k_gain.reshape(1, HEAD_DIM))


def _attn_step(i, group, hh, out_rows, slope_ref, qa_ref, kt_ref, va_ref, o_ref):
    bs = MOBA_BLOCK
    c1 = HEAD_DIM ** -0.5 * LOG2E
    hcols = slice(hh * HEAD_DIM, (hh + 1) * HEAD_DIM)
    own = slice(i * bs, (i + 1) * bs)
    q_aug = qa_ref[hh, own, :]
    slope2 = slope_ref[hh] * LOG2E
    key_off = lax.broadcasted_iota(jnp.int32, (1, bs), 1)

    def col_bias(j):
        return slope2 * ((j - i) * bs + key_off).astype(F32)

    row = lax.broadcasted_iota(jnp.int32, (bs, bs), 0)
    col = lax.broadcasted_iota(jnp.int32, (bs, bs), 1)
    s = jnp.dot(q_aug[:, :HEAD_DIM], kt_ref[hh, :HEAD_DIM, own], preferred_element_type=F32)
    t_own = jnp.where(row >= col, s * c1 + col_bias(i), -jnp.inf)
    va_own = va_ref[hh, own, :]
    n_past = i

    def row_max(ts):
        tmax = functools.reduce(jnp.maximum, ts)
        return jnp.max(jnp.maximum(tmax[:, :LANES], tmax[:, LANES:]), axis=1, keepdims=True)

    m = acc = None
    n_groups = max(1, (n_past + group // 2) // group)
    bounds = [n_past * g // n_groups for g in range(n_groups + 1)]
    for j0, j1 in zip(bounds[:-1], bounds[1:]):
        ts = []
        if j1 > j0:
            s = jnp.dot(q_aug, kt_ref[hh, :, j0 * bs:j1 * bs], preferred_element_type=F32)
            ts = [s[:, (j - j0) * bs:(j - j0 + 1) * bs] * c1 + col_bias(j) for j in range(j0, j1)]
        if j0 == 0:
            m = row_max(ts + [t_own])
            acc = jnp.dot(jnp.exp2(t_own - m).astype(BF16), va_own, preferred_element_type=F32)
        else:
            m_new = jnp.maximum(m, row_max(ts))
            acc = jnp.exp2(m - m_new) * acc
            m = m_new
        if ts:
            pcat = jnp.concatenate([jnp.exp2(t - m).astype(BF16) for t in ts], axis=1)
            acc = acc + jnp.dot(pcat, va_ref[hh, j0 * bs:j1 * bs, :], preferred_element_type=F32)

    o_ref[out_rows, hcols] = (acc[:, :HEAD_DIM] / acc[:, HEAD_DIM:]).astype(o_ref.dtype)


def _attn_prepare(hh, n_blocks, q_ref, k_ref, v_ref, qa_ref, kt_ref, va_ref, kmean_ref):
    bs = MOBA_BLOCK
    seq = q_ref.shape[0]
    hcols = slice(hh * HEAD_DIM, (hh + 1) * HEAD_DIM)
    va_ref[hh, :, :HEAD_DIM] = v_ref[:, hcols]
    va_ref[hh, :, HEAD_DIM:] = jnp.ones((seq, LANES), BF16)
    feat = lax.broadcasted_iota(jnp.int32, (LANES, bs), 0)
    for j in range(n_blocks):
        kj = k_ref[j * bs:(j + 1) * bs, hcols].astype(F32)
        kmean_ref[j:j + 1, :] = jnp.mean(kj, axis=0, keepdims=True)
        kt_ref[hh, :HEAD_DIM, j * bs:(j + 1) * bs] = kj.T.astype(BF16)
        kt_ref[hh, HEAD_DIM:, j * bs:(j + 1) * bs] = jnp.where(feat == j, 1.0, 0.0).astype(BF16)

    q32 = q_ref[:, hcols]
    q_hi = q32.astype(BF16)
    q_lo = (q32 - q_hi.astype(F32)).astype(BF16)
    km = kmean_ref[...]
    km_hi = km.astype(BF16)
    km_lo = (km - km_hi.astype(F32)).astype(BF16)

    def dot_nt(a, b):
        return lax.dot_general(a, b, NT_DIMS, preferred_element_type=F32)

    gate = dot_nt(km_hi, q_hi) + (dot_nt(km_hi, q_lo) + dot_nt(km_lo, q_hi))
    blk = lax.broadcasted_iota(jnp.int32, gate.shape, 0)
    own = lax.broadcasted_iota(jnp.int32, gate.shape, 1) // bs
    rank = jnp.zeros(gate.shape, jnp.int32)
    for jp in range(n_blocks - 1):
        gj = gate[jp:jp + 1, :]
        beats = jnp.logical_or(gj > gate, jnp.logical_and(gj == gate, blk > jp))
        rank = rank + jnp.where(jnp.logical_and(beats, jp < own), 1, 0)
    selected = jnp.logical_and(blk < own, rank < MOBA_TOPK)
    bias_t = jnp.where(selected, 0.0, MASK_BIAS).astype(F32)
    pad = jnp.zeros((LANES - n_blocks, bs), F32)
    for ib in range(n_blocks):
        rows = slice(ib * bs, (ib + 1) * bs)
        chunk = jnp.concatenate([bias_t[:, rows], pad], axis=0)
        qa_ref[hh, rows, :HEAD_DIM] = q_hi[rows, :]
        qa_ref[hh, rows, HEAD_DIM:] = chunk.T.astype(BF16)


def _attn_kernel(slope_ref, q_ref, k_ref, v_ref, o_ref, qa_ref, kt_ref, va_ref, kmean_ref, *, n_blocks,
                 group):
    bs = MOBA_BLOCK
    step = pl.program_id(2)
    heads = q_ref.shape[1] // HEAD_DIM
    q_blocks = o_ref.shape[0] // bs

    @pl.when(step == 0)
    def _():
        for hh in range(heads):
            _attn_prepare(hh, n_blocks, q_ref, k_ref, v_ref, qa_ref, kt_ref, va_ref, kmean_ref)

    for c in range(n_blocks // q_blocks):
        @pl.when(step == c)
        def _(c=c):
            for sub in range(q_blocks):
                for hh in range(heads):
                    _attn_step(c * q_blocks + sub, group, hh, slice(sub * bs, (sub + 1) * bs), slope_ref,
                               qa_ref, kt_ref, va_ref, o_ref)


def _moba_attention(q, k, v, batch, seq):
    t, d = q.shape
    bs = MOBA_BLOCK
    n_blocks = seq // bs
    slopes = np.asarray(2.0 ** (-8.0 * np.arange(1, N_HEADS + 1) / N_HEADS), dtype=np.float32)
    slopes = jnp.asarray(np.broadcast_to(slopes[:, None, None], (N_HEADS, 1, bs)))
    kern = functools.partial(_attn_kernel, n_blocks=n_blocks, group=ATTN_GROUP)
    hps = ATTN_HEADS_PER_STEP
    hw = hps * HEAD_DIM
    qbs = ATTN_QBLOCKS_PER_STEP
    n_steps = n_blocks // qbs
    return pl.pallas_call(
        kern,
        grid=(batch, N_HEADS // hps, n_steps),
        in_specs=[
            pl.BlockSpec((hps, 1, bs), lambda b, h, i: (h, 0, 0)),
            pl.BlockSpec((seq, hw), lambda b, h, i: (b, h)),
            pl.BlockSpec((seq, hw), lambda b, h, i: (b, h)),
            pl.BlockSpec((seq, hw), lambda b, h, i: (b, h)),
        ],
        out_specs=pl.BlockSpec((qbs * bs, hw), lambda b, h, i: (b * n_steps + i, h)),
        out_shape=jax.ShapeDtypeStruct((t, d), BF16),
        scratch_shapes=[
            pltpu.VMEM((hps, seq, HEAD_DIM + LANES), BF16),
            pltpu.VMEM((hps, HEAD_DIM + LANES, seq), BF16),
            pltpu.VMEM((hps, seq, HEAD_DIM + LANES), BF16),
            pltpu.VMEM((n_blocks, HEAD_DIM), F32),
        ],
        compiler_params=_params("parallel", "parallel", "arbitrary"),
        name="moba_attn",
    )(slopes, q, k, v)


def _proj_res_kernel(x_ref, a_ref, w_ref, o_ref):
    a = a_ref[...]
    for c0 in range(0, w_ref.shape[1], SUB_N):
        cols = slice(c0, c0 + SUB_N)
        o_ref[:, cols] = x_ref[:, cols] + jnp.dot(a, w_ref[:, cols], preferred_element_type=F32)


def _proj_residual(x, a, w, layer):
    t, n = x.shape
    kdim = a.shape[1]
    tm = 512
    return pl.pallas_call(
        _proj_res_kernel,
        grid=(t // tm,),
        in_specs=[
            pl.BlockSpec((tm, n), lambda i: (i, 0)),
            pl.BlockSpec((tm, kdim), lambda i: (i, 0)),
            pl.BlockSpec((None, kdim, n), lambda i: (layer, 0, 0)),
        ],
        out_specs=pl.BlockSpec((tm, n), lambda i: (i, 0)),
        out_shape=jax.ShapeDtypeStruct((t, n), F32),
        compiler_params=_params("parallel"),
        name="proj_residual",
    )(x, a, w)


def _pool_kernel(x_ref, halo_ref, g_ref, w_ref, sc_ref, o_ref, *, ts):
    i = pl.program_id(1)
    x = x_ref[...]
    gain = g_ref[...]
    h = _rms_rows(x, gain)
    h_halo = jnp.where(i == 0, 0.0, _rms_rows(halo_ref[...], gain))
    he = jnp.concatenate([h_halo, h], axis=0)
    pos = i * ts + lax.broadcasted_iota(jnp.int32, (ts, 1), 0)
    count = (pos + 1).astype(F32)
    cg = POOL_GROUP_DIM
    outs = []
    for g, win in enumerate(POOL_WINDOWS):
        a = he[:, g * cg:(g + 1) * cg]
        lead = 0
        width = 1
        while width < win:
            a = a[width:, :] + a[:-width, :]
            lead += width
            width *= 2
        start = POOL_HALO - lead
        pooled = a[start:start + ts, :] / jnp.minimum(count, float(win)) - h[:, g * cg:(g + 1) * cg]
        outs.append(jnp.dot(pooled.astype(BF16), w_ref[g], preferred_element_type=F32))
    y = jnp.concatenate(outs, axis=1)
    o_ref[...] = x + y * sc_ref[...]


def _pool_mixer(x, gain, w_group, scale, batch, seq):
    t, d = x.shape
    ts = 512
    per_seq = seq // ts
    halo_per_tile = ts // POOL_HALO
    kern = functools.partial(_pool_kernel, ts=ts)

    def halo_map(b, i):
        return (jnp.maximum((b * per_seq + i) * halo_per_tile - 1, 0), 0)

    return pl.pallas_call(
        kern,
        grid=(batch, per_seq),
        in_specs=[
            pl.BlockSpec((ts, d), lambda b, i: (b * per_seq + i, 0)),
            pl.BlockSpec((POOL_HALO, d), halo_map),
            pl.BlockSpec((1, d), lambda b, i: (0, 0)),
            pl.BlockSpec((N_POOL_GROUPS, POOL_GROUP_DIM, POOL_GROUP_DIM), lambda b, i: (0, 0, 0)),
            pl.BlockSpec((1, d), lambda b, i: (0, 0)),
        ],
        out_specs=pl.BlockSpec((ts, d), lambda b, i: (b * per_seq + i, 0)),
        out_shape=jax.ShapeDtypeStruct((t, d), F32),
        compiler_params=_params("parallel", "parallel"),
        name="pool_mixer",
    )(x, x, gain.reshape(1, d), w_group.astype(BF16), scale.reshape(1, d))


def _sgu_in_kernel(x_ref, g_ref, w_ref, z_ref, ssq_ref, h_ref, *, n_u):
    j = pl.program_id(1)

    def project(with_ssq, first=False):
        if first:
            h = _rms_rows(x_ref[...], g_ref[...]).astype(BF16)
            h_ref[...] = h
            ssq_ref[...] = jnp.zeros_like(ssq_ref)
        else:
            h = h_ref[...]
        part = None
        for c0 in range(0, w_ref.shape[1], SUB_N):
            y = jnp.dot(h, w_ref[:, c0:c0 + SUB_N], preferred_element_type=F32)
            z = 0.5 * y * (1.0 + lax.erf(y * np.float32(np.sqrt(0.5))))
            z_ref[:, c0:c0 + SUB_N] = z.astype(BF16)
            if with_ssq:
                zz = z * z
                for c in range(SUB_N // LANES):
                    sl = zz[:, c * LANES:(c + 1) * LANES]
                    part = sl if part is None else part + sl
        if with_ssq:
            ssq_ref[...] += part

    pl.when(j == 0)(functools.partial(project, False, first=True))
    pl.when(jnp.logical_and(j > 0, j < n_u))(functools.partial(project, False))
    pl.when(j >= n_u)(functools.partial(project, True))


def _sgu_in(x, gain, w_in, layer):
    t, d = x.shape
    n = w_in.shape[2]
    tm, tn = 512, 2048
    n_u = (n // 2) // tn
    kern = functools.partial(_sgu_in_kernel, n_u=n_u)
    return pl.pallas_call(
        kern,
        grid=(t // tm, n // tn),
        in_specs=[
            pl.BlockSpec((tm, d), lambda i, j: (i, 0)),
            pl.BlockSpec((1, d), lambda i, j: (0, 0)),
            pl.BlockSpec((None, d, tn), lambda i, j: (layer, 0, j)),
        ],
        out_specs=[
            pl.BlockSpec((tm, tn), lambda i, j: (i, j)),
            pl.BlockSpec((tm, LANES), lambda i, j: (i, 0)),
        ],
        out_shape=[
            jax.ShapeDtypeStruct((t, n), BF16),
            jax.ShapeDtypeStruct((t, LANES), F32),
        ],
        scratch_shapes=[pltpu.VMEM((tm, d), BF16)],
        compiler_params=_params("parallel", "arbitrary"),
        name="sgu_in",
    )(x, gain.reshape(1, d), w_in)


def _sgu_out_kernel(x_ref, u_ref, v_ref, ssq_ref, vg_ref, ws_ref, bs_ref, wo_ref, o_ref, *, tm):
    g = pl.program_id(1)

    @pl.when(g == 0)
    def _():
        o_ref[...] = x_ref[...]

    inv = lax.rsqrt(jnp.sum(ssq_ref[...], axis=1, keepdims=True) / SGU_DIM + RMS_EPS)
    row = lax.broadcasted_iota(jnp.int32, (SGU_CHUNK, SGU_CHUNK), 0)
    col = lax.broadcasted_iota(jnp.int32, (SGU_CHUNK, SGU_CHUNK), 1)
    gd = SGU_GROUP_DIM
    acc = None
    for gg in range(ws_ref.shape[0]):
        gcols = slice(gg * gd, (gg + 1) * gd)
        vn = (v_ref[:, gcols].astype(F32) * inv * vg_ref[:, gcols]).astype(BF16)
        w_s = jnp.where(row >= col, ws_ref[gg], 0.0).astype(BF16)
        bias = bs_ref[gg][:, :1]
        parts = []
        for c in range(tm // SGU_CHUNK):
            vc = vn[c * SGU_CHUNK:(c + 1) * SGU_CHUNK, :]
            parts.append(jnp.dot(w_s, vc, preferred_element_type=F32) + bias)
        sv = jnp.concatenate(parts, axis=0)
        p = (u_ref[:, gcols].astype(F32) * sv).astype(BF16)
        y = jnp.dot(p, wo_ref[gcols, :], preferred_element_type=F32)
        acc = y if acc is None else acc + y
    o_ref[...] += acc


def _sgu_out(x, z, ssq, v_gain, w_spatial, b_spatial, w_out, layer):
    t, d = x.shape
    tm = 512
    gps = SGU_GROUPS_PER_STEP
    gw = gps * SGU_GROUP_DIM
    n_steps = SGU_GROUPS // gps
    b_rep = jnp.broadcast_to(b_spatial[:, :, None], (SGU_GROUPS, SGU_CHUNK, LANES))
    kern = functools.partial(_sgu_out_kernel, tm=tm)
    return pl.pallas_call(
        kern,
        grid=(t // tm, n_steps),
        in_specs=[
            pl.BlockSpec((tm, d), lambda i, g: (i, 0)),
            pl.BlockSpec((tm, gw), lambda i, g: (i, g)),
            pl.BlockSpec((tm, gw), lambda i, g: (i, n_steps + g)),
            pl.BlockSpec((tm, LANES), lambda i, g: (i, 0)),
            pl.BlockSpec((1, gw), lambda i, g: (0, g)),
            pl.BlockSpec((gps, SGU_CHUNK, SGU_CHUNK), lambda i, g: (g, 0, 0)),
            pl.BlockSpec((gps, SGU_CHUNK, LANES), lambda i, g: (g, 0, 0)),
            pl.BlockSpec((None, gw, d), lambda i, g: (layer, g, 0)),
        ],
        out_specs=pl.BlockSpec((tm, d), lambda i, g: (i, 0)),
        out_shape=jax.ShapeDtypeStruct((t, d), F32),
        compiler_params=_params("parallel", "arbitrary"),
        name="sgu_out",
    )(x, z, z, ssq, v_gain.reshape(1, SGU_DIM), w_spatial, b_rep, w_out)


def kernel(x, ffn1_norm, ffn1_w_gate, ffn1_w_up, ffn1_w_down, mix_norm, ffn2_norm, ffn2_w_gate, ffn2_w_up, ffn2_w_down, attn_w_qkv, attn_q_gain, attn_k_gain, attn_w_out, pool_w_group, pool_scale, sgu_w_in, sgu_v_gain, sgu_w_spatial, sgu_b_spatial, sgu_w_out):
    batch, seq, d = x.shape
    assert d == D_MODEL and seq % MOBA_BLOCK == 0 and seq % SGU_CHUNK == 0
    xt = x.reshape(batch * seq, d)
    ffn1_f32 = (ffn1_w_gate, ffn1_w_up, ffn1_w_down)
    ffn2_f32 = (ffn2_w_gate, ffn2_w_up, ffn2_w_down)
    w_ffn = tuple(w[0].astype(BF16) for w in ffn1_f32)
    w_qkv, w_attn_out = attn_w_qkv.astype(BF16), attn_w_out.astype(BF16)
    w_sgu_in, w_sgu_out = sgu_w_in.astype(BF16), sgu_w_out.astype(BF16)
    for i in range(DEPTH):
        xt, w_ffn = _ffn(xt, ffn1_norm[i], w_ffn, ffn2_f32, i)
        kind, j = i % N_MIXERS, i // N_MIXERS
        if kind == 0:
            q, k, v = _qkv_proj(xt, mix_norm[i], w_qkv, j, attn_q_gain[j], attn_k_gain[j])
            o = _moba_attention(q, k, v, batch, seq)
            xt = _proj_residual(xt, o, w_attn_out, j)
        elif kind == 1:
            xt = _pool_mixer(xt, mix_norm[i], pool_w_group[j], pool_scale[j], batch, seq)
        else:
            z, ssq = _sgu_in(xt, mix_norm[i], w_sgu_in, j)
            xt = _sgu_out(xt, z, ssq, sgu_v_gain[j], sgu_w_spatial[j], sgu_b_spatial[j], w_sgu_out, j)
        if i + 1 < DEPTH:
            xt, w_ffn = _ffn(xt, ffn2_norm[i], w_ffn, ffn1_f32, i + 1)
        else:
            xt, _ = _ffn(xt, ffn2_norm[i], w_ffn)
    return xt.reshape(batch, seq, d)
```

```python
import functools

import jax
import jax.numpy as jnp
import numpy as np
from jax import lax
from jax.experimental import pallas as pl
from jax.experimental.pallas import tpu as pltpu

D_MODEL = 2048
DEPTH = 4
N_MIXERS = 3
RMS_EPS = 1e-6
D_FF = 5504
N_HEADS = 16
HEAD_DIM = D_MODEL // N_HEADS
MOBA_BLOCK = 256
MOBA_TOPK = 3
POOL_WINDOWS = (2, 4, 8, 16)
N_POOL_GROUPS = len(POOL_WINDOWS)
POOL_GROUP_DIM = D_MODEL // N_POOL_GROUPS
POOL_HALO = max(POOL_WINDOWS)
SGU_DIM = 3 * D_MODEL
SGU_CHUNK = 128
SGU_GROUPS = 8
SGU_GROUP_DIM = SGU_DIM // SGU_GROUPS

LANES = 128
SUBLANES = 8
MXU_DIM = 256
VMEM_LIMIT = 58 * 1024 * 1024

F32 = jnp.float32
BF16 = jnp.bfloat16
MASK_BIAS = -1e9
LOG2E = float(np.log2(np.e))
ATTN_GROUP = 4
ATTN_HEADS_PER_STEP = 2
ATTN_QBLOCKS_PER_STEP = 2
SGU_GROUPS_PER_STEP = 2
NT_DIMS = (((1,), (1,)), ((), ()))


SUB_N = 2 * MXU_DIM
FFN_TM = 1024
FFN_TF = 512


def _params(*sem):
    return pltpu.CompilerParams(dimension_semantics=sem, vmem_limit_bytes=VMEM_LIMIT)


def _rms_rows(x, gain):
    ms = jnp.mean(x * x, axis=-1, keepdims=True)
    return x * lax.rsqrt(ms + RMS_EPS) * gain


def _ffn_kernel(*refs, convert_next):
    if convert_next:
        x_ref, g_ref, wg_ref, wu_ref, wd_ref, *next_f32, o_ref, og_ref, ou_ref, od_ref, h_ref = refs
        for src, dst in zip(next_f32, (og_ref, ou_ref, od_ref)):
            dst[...] = src[...].astype(BF16)
    else:
        x_ref, g_ref, wg_ref, wu_ref, wd_ref, o_ref, h_ref = refs
    f = pl.program_id(1)
    tf = wd_ref.shape[0]

    def step(first):
        if first:
            x = x_ref[...]
            h = _rms_rows(x, g_ref[...]).astype(BF16)
            h_ref[...] = h
        else:
            h = h_ref[...]
        gate = jnp.dot(h, wg_ref[...], preferred_element_type=F32)
        up = jnp.dot(h, wu_ref[...], preferred_element_type=F32)
        a = (gate * jax.nn.sigmoid(gate)) * up * 0.5
        valid = D_FF - f * tf
        a = jnp.where(lax.broadcasted_iota(jnp.int32, (1, tf), 1) < valid, a, 0.0)
        wd = wd_ref[...]
        wd = jnp.where(lax.broadcasted_iota(jnp.int32, (tf, 1), 0) < valid, wd, jnp.zeros_like(wd))
        y = jnp.dot(a.astype(BF16), wd, preferred_element_type=F32)
        if first:
            o_ref[...] = x + y
        else:
            o_ref[...] += y

    pl.when(f == 0)(functools.partial(step, True))
    pl.when(f > 0)(functools.partial(step, False))


def _ffn(x, gain, weights, next_f32=None, next_layer=None):
    t, d = x.shape
    tm, tf = FFN_TM, FFN_TF
    n_tiles = t // tm
    in_specs = [
        pl.BlockSpec((tm, d), lambda i, f: (i, 0)),
        pl.BlockSpec((1, d), lambda i, f: (0, 0)),
        pl.BlockSpec((d, tf), lambda i, f: (0, f)),
        pl.BlockSpec((d, tf), lambda i, f: (0, f)),
        pl.BlockSpec((tf, d), lambda i, f: (f, 0)),
    ]
    out_specs = [pl.BlockSpec((tm, d), lambda i, f: (i, 0))]
    out_shape = [jax.ShapeDtypeStruct((t, d), F32)]
    operands = [x, gain.reshape(1, d), *weights]
    if next_f32 is not None:
        rows = d // n_tiles
        in_specs += [
            pl.BlockSpec((None, rows, tf), lambda i, f: (next_layer, i, f)),
            pl.BlockSpec((None, rows, tf), lambda i, f: (next_layer, i, f)),
            pl.BlockSpec((None, tf, rows), lambda i, f: (next_layer, f, i)),
        ]
        out_specs += [
            pl.BlockSpec((rows, tf), lambda i, f: (i, f)),
            pl.BlockSpec((rows, tf), lambda i, f: (i, f)),
            pl.BlockSpec((tf, rows), lambda i, f: (f, i)),
        ]
        out_shape += [jax.ShapeDtypeStruct(w.shape[1:], BF16) for w in next_f32]
        operands += list(next_f32)
    outs = pl.pallas_call(
        functools.partial(_ffn_kernel, convert_next=next_f32 is not None),
        grid=(n_tiles, pl.cdiv(D_FF, tf)),
        in_specs=in_specs,
        out_specs=out_specs,
        out_shape=out_shape,
        scratch_shapes=[pltpu.VMEM((tm, d), BF16)],
        compiler_params=_params("parallel", "arbitrary"),
        name="ffn",
    )(*operands)
    return outs[0], tuple(outs[1:])


def _qkv_kernel(x_ref, g_ref, w_ref, qg_ref, kg_ref, q_ref, k_ref, v_ref, h_ref):
    j = pl.program_id(1)

    def head_norm(y, gain):
        cols = []
        for hh in range(y.shape[1] // HEAD_DIM):
            cols.append(_rms_rows(y[:, hh * HEAD_DIM:(hh + 1) * HEAD_DIM], gain))
        return jnp.concatenate(cols, axis=1)

    def project(out_ref, epilogue, first=False):
        if first:
            h = _rms_rows(x_ref[...], g_ref[...]).astype(BF16)
            h_ref[...] = h
        else:
            h = h_ref[...]
        for c0 in range(0, w_ref.shape[1], SUB_N):
            y = jnp.dot(h, w_ref[:, c0:c0 + SUB_N], preferred_element_type=F32)
            out_ref[:, c0:c0 + SUB_N] = epilogue(y).astype(out_ref.dtype)

    @pl.when(j == 0)
    def _():
        project(q_ref, lambda y: head_norm(y, qg_ref[...]), first=True)

    @pl.when(j == 1)
    def _():
        project(k_ref, lambda y: head_norm(y, kg_ref[...]))

    @pl.when(j == 2)
    def _():
        project(v_ref, lambda y: y)


def _qkv_proj(x, gain, w, layer, q_gain, k_gain):
    t, d = x.shape
    tm = 512
    return pl.pallas_call(
        _qkv_kernel,
        grid=(t // tm, 3),
        in_specs=[
            pl.BlockSpec((tm, d), lambda i, j: (i, 0)),
            pl.BlockSpec((1, d), lambda i, j: (0, 0)),
            pl.BlockSpec((None, d, d), lambda i, j: (layer, 0, j)),
            pl.BlockSpec((1, HEAD_DIM), lambda i, j: (0, 0)),
            pl.BlockSpec((1, HEAD_DIM), lambda i, j: (0, 0)),
        ],
        out_specs=[
            pl.BlockSpec((tm, d), lambda i, j: (i, 0)),
            pl.BlockSpec((tm, d), lambda i, j: (i, 0)),
            pl.BlockSpec((tm, d), lambda i, j: (i, 0)),
        ],
        out_shape=[
            jax.ShapeDtypeStruct((t, d), F32),
            jax.ShapeDtypeStruct((t, d), BF16),
            jax.ShapeDtypeStruct((t, d), BF16),
        ],
        scratch_shapes=[pltpu.VMEM((tm, d), BF16)],
        compiler_params=_params("parallel", "arbitrary"),
        name="qkv_proj",
    )(x, gain.reshape(1, d), w, q_gain.reshape(1, HEAD_DIM), k_gain.reshape(1, HEAD_DIM))


def _attn_step(i, group, hh, out_rows, slope_ref, qa_ref, kt_ref, va_ref, o_ref):
    bs = MOBA_BLOCK
    c1 = HEAD_DIM ** -0.5 * LOG2E
    hcols = slice(hh * HEAD_DIM, (hh + 1) * HEAD_DIM)
    own = slice(i * bs, (i + 1) * bs)
    q_aug = qa_ref[hh, own, :]
    slope2 = slope_ref[hh] * LOG2E
    key_off = lax.broadcasted_iota(jnp.int32, (1, bs), 1)

    def col_bias(j):
        return slope2 * ((j - i) * bs + key_off).astype(F32)

    row = lax.broadcasted_iota(jnp.int32, (bs, bs), 0)
    col = lax.broadcasted_iota(jnp.int32, (bs, bs), 1)
    s = jnp.dot(q_aug[:, :HEAD_DIM], kt_ref[hh, :HEAD_DIM, own], preferred_element_type=F32)
    t_own = jnp.where(row >= col, s * c1 + col_bias(i), -jnp.inf)
    va_own = va_ref[hh, own, :]
    n_past = i

    def row_max(ts):
        tmax = functools.reduce(jnp.maximum, ts)
        return jnp.max(jnp.maximum(tmax[:, :LANES], tmax[:, LANES:]), axis=1, keepdims=True)

    m = acc = None
    n_groups = max(1, (n_past + group // 2) // group)
    bounds = [n_past * g // n_groups for g in range(n_groups + 1)]
    for j0, j1 in zip(bounds[:-1], bounds[1:]):
        ts = []
        if j1 > j0:
            s = jnp.dot(q_aug, kt_ref[hh, :, j0 * bs:j1 * bs], preferred_element_type=F32)
            ts = [s[:, (j - j0) * bs:(j - j0 + 1) * bs] * c1 + col_bias(j) for j in range(j0, j1)]
        if j0 == 0:
            m = row_max(ts + [t_own])
            acc = jnp.dot(jnp.exp2(t_own - m).astype(BF16), va_own, preferred_element_type=F32)
        else:
            m_new = jnp.maximum(m, row_max(ts))
            acc = jnp.exp2(m - m_new) * acc
            m = m_new
        if ts:
            pcat = jnp.concatenate([jnp.exp2(t - m).astype(BF16) for t in ts], axis=1)
            acc = acc + jnp.dot(pcat, va_ref[hh, j0 * bs:j1 * bs, :], preferred_element_type=F32)

    o_ref[out_rows, hcols] = (acc[:, :HEAD_DIM] / acc[:, HEAD_DIM:]).astype(o_ref.dtype)


def _attn_prepare_keys(hh, j, k_ref, v_ref, kt_ref, va_ref, kmean_ref):
    bs = MOBA_BLOCK
    hcols = slice(hh * HEAD_DIM, (hh + 1) * HEAD_DIM)
    rows = slice(j * bs, (j + 1) * bs)
    va_ref[hh, rows, :HEAD_DIM] = v_ref[rows, hcols]
    va_ref[hh, rows, HEAD_DIM:] = jnp.ones((bs, LANES), BF16)
    kj = k_ref[rows, hcols].astype(F32)
    kmean_ref[hh, j:j + 1, :] = jnp.mean(kj, axis=0, keepdims=True)
    kt_ref[hh, :HEAD_DIM, rows] = kj.T.astype(BF16)
    feat = lax.broadcasted_iota(jnp.int32, (LANES, bs), 0)
    kt_ref[hh, HEAD_DIM:, rows] = jnp.where(feat == j, 1.0, 0.0).astype(BF16)


def _attn_prepare_queries(hh, i, q_ref, qa_ref, kmean_ref):
    bs = MOBA_BLOCK
    hcols = slice(hh * HEAD_DIM, (hh + 1) * HEAD_DIM)
    rows = slice(i * bs, (i + 1) * bs)
    q32 = q_ref[rows, hcols]
    q_hi = q32.astype(BF16)
    qa_ref[hh, rows, :HEAD_DIM] = q_hi
    if i == 0:
        qa_ref[hh, rows, HEAD_DIM:] = jnp.zeros((bs, LANES), BF16)
        return
    n_rows = -(-i // SUBLANES) * SUBLANES
    q_lo = (q32 - q_hi.astype(F32)).astype(BF16)
    km = kmean_ref[hh, :n_rows, :]
    km_hi = km.astype(BF16)
    km_lo = (km - km_hi.astype(F32)).astype(BF16)

    def dot_nt(a, b):
        return lax.dot_general(a, b, NT_DIMS, preferred_element_type=F32)

    gate = dot_nt(km_hi, q_hi) + (dot_nt(km_hi, q_lo) + dot_nt(km_lo, q_hi))
    blk = lax.broadcasted_iota(jnp.int32, gate.shape, 0)
    rank = jnp.zeros(gate.shape, jnp.int32)
    for jp in range(i):
        gj = gate[jp:jp + 1, :]
        beats = jnp.logical_or(gj > gate, jnp.logical_and(gj == gate, blk > jp))
        rank = rank + jnp.where(beats, 1, 0)
    selected = jnp.logical_and(blk < i, rank < MOBA_TOPK)
    bias_t = jnp.where(selected, 0.0, MASK_BIAS).astype(F32)
    bias_t = jnp.concatenate([bias_t, jnp.full((LANES - n_rows, bs), MASK_BIAS, F32)], axis=0)
    qa_ref[hh, rows, HEAD_DIM:] = bias_t.T.astype(BF16)


def _attn_kernel(slope_ref, q_ref, k_ref, v_ref, o_ref, qa_ref, kt_ref, va_ref, kmean_ref, *, n_blocks,
                 group):
    bs = MOBA_BLOCK
    step = pl.program_id(2)
    heads = q_ref.shape[1] // HEAD_DIM
    q_blocks = o_ref.shape[0] // bs

    def prepare(blocks):
        for i in blocks:
            for hh in range(heads):
                _attn_prepare_keys(hh, i, k_ref, v_ref, kt_ref, va_ref, kmean_ref)
                _attn_prepare_queries(hh, i, q_ref, qa_ref, kmean_ref)

    for c in range(n_blocks // q_blocks):
        @pl.when(step == c)
        def _(c=c):
            first = c * q_blocks
            if c == 0:
                kmean_ref[...] = jnp.zeros_like(kmean_ref)
                prepare(range(q_blocks))
            for sub in range(q_blocks):
                for hh in range(heads):
                    _attn_step(first + sub, group, hh, slice(sub * bs, (sub + 1) * bs), slope_ref,
                               qa_ref, kt_ref, va_ref, o_ref)
            prepare(range(first + q_blocks, min(first + 2 * q_blocks, n_blocks)))


def _moba_attention(q, k, v, batch, seq):
    t, d = q.shape
    bs = MOBA_BLOCK
    n_blocks = seq // bs
    slopes = np.asarray(2.0 ** (-8.0 * np.arange(1, N_HEADS + 1) / N_HEADS), dtype=np.float32)
    slopes = jnp.asarray(np.broadcast_to(slopes[:, None, None], (N_HEADS, 1, bs)))
    kern = functools.partial(_attn_kernel, n_blocks=n_blocks, group=ATTN_GROUP)
    hps = ATTN_HEADS_PER_STEP
    hw = hps * HEAD_DIM
    qbs = ATTN_QBLOCKS_PER_STEP
    n_steps = n_blocks // qbs
    return pl.pallas_call(
        kern,
        grid=(batch, N_HEADS // hps, n_steps),
        in_specs=[
            pl.BlockSpec((hps, 1, bs), lambda b, h, i: (h, 0, 0)),
            pl.BlockSpec((seq, hw), lambda b, h, i: (b, h)),
            pl.BlockSpec((seq, hw), lambda b, h, i: (b, h)),
            pl.BlockSpec((seq, hw), lambda b, h, i: (b, h)),
        ],
        out_specs=pl.BlockSpec((qbs * bs, hw), lambda b, h, i: (b * n_steps + i, h)),
        out_shape=jax.ShapeDtypeStruct((t, d), BF16),
        scratch_shapes=[
            pltpu.VMEM((hps, seq, HEAD_DIM + LANES), BF16),
            pltpu.VMEM((hps, HEAD_DIM + LANES, seq), BF16),
            pltpu.VMEM((hps, seq, HEAD_DIM + LANES), BF16),
            pltpu.VMEM((hps, n_blocks, HEAD_DIM), F32),
        ],
        compiler_params=_params("parallel", "parallel", "arbitrary"),
        name="moba_attn",
    )(slopes, q, k, v)


def _proj_res_kernel(x_ref, a_ref, w_ref, o_ref):
    a = a_ref[...]
    for c0 in range(0, w_ref.shape[1], SUB_N):
        cols = slice(c0, c0 + SUB_N)
        o_ref[:, cols] = x_ref[:, cols] + jnp.dot(a, w_ref[:, cols], preferred_element_type=F32)


def _proj_residual(x, a, w, layer):
    t, n = x.shape
    kdim = a.shape[1]
    tm = 512
    return pl.pallas_call(
        _proj_res_kernel,
        grid=(t // tm,),
        in_specs=[
            pl.BlockSpec((tm, n), lambda i: (i, 0)),
            pl.BlockSpec((tm, kdim), lambda i: (i, 0)),
            pl.BlockSpec((None, kdim, n), lambda i: (layer, 0, 0)),
        ],
        out_specs=pl.BlockSpec((tm, n), lambda i: (i, 0)),
        out_shape=jax.ShapeDtypeStruct((t, n), F32),
        compiler_params=_params("parallel"),
        name="proj_residual",
    )(x, a, w)


def _pool_kernel(x_ref, halo_ref, g_ref, w_ref, sc_ref, o_ref, *, ts):
    i = pl.program_id(1)
    x = x_ref[...]
    gain = g_ref[...]
    h = _rms_rows(x, gain)
    h_halo = jnp.where(i == 0, 0.0, _rms_rows(halo_ref[...], gain))
    he = jnp.concatenate([h_halo, h], axis=0)
    pos = i * ts + lax.broadcasted_iota(jnp.int32, (ts, 1), 0)
    count = (pos + 1).astype(F32)
    cg = POOL_GROUP_DIM
    outs = []
    for g, win in enumerate(POOL_WINDOWS):
        a = he[:, g * cg:(g + 1) * cg]
        lead = 0
        width = 1
        while width < win:
            a = a[width:, :] + a[:-width, :]
            lead += width
            width *= 2
        start = POOL_HALO - lead
        pooled = a[start:start + ts, :] / jnp.minimum(count, float(win)) - h[:, g * cg:(g + 1) * cg]
        outs.append(jnp.dot(pooled.astype(BF16), w_ref[g], preferred_element_type=F32))
    y = jnp.concatenate(outs, axis=1)
    o_ref[...] = x + y * sc_ref[...]


def _pool_mixer(x, gain, w_group, scale, batch, seq):
    t, d = x.shape
    ts = 512
    per_seq = seq // ts
    halo_per_tile = ts // POOL_HALO
    kern = functools.partial(_pool_kernel, ts=ts)

    def halo_map(b, i):
        return (jnp.maximum((b * per_seq + i) * halo_per_tile - 1, 0), 0)

    return pl.pallas_call(
        kern,
        grid=(batch, per_seq),
        in_specs=[
            pl.BlockSpec((ts, d), lambda b, i: (b * per_seq + i, 0)),
            pl.BlockSpec((POOL_HALO, d), halo_map),
            pl.BlockSpec((1, d), lambda b, i: (0, 0)),
            pl.BlockSpec((N_POOL_GROUPS, POOL_GROUP_DIM, POOL_GROUP_DIM), lambda b, i: (0, 0, 0)),
            pl.BlockSpec((1, d), lambda b, i: (0, 0)),
        ],
        out_specs=pl.BlockSpec((ts, d), lambda b, i: (b * per_seq + i, 0)),
        out_shape=jax.ShapeDtypeStruct((t, d), F32),
        compiler_params=_params("parallel", "parallel"),
        name="pool_mixer",
    )(x, x, gain.reshape(1, d), w_group.astype(BF16), scale.reshape(1, d))


def _sgu_in_kernel(x_ref, g_ref, w_ref, z_ref, ssq_ref, h_ref, *, n_u):
    j = pl.program_id(1)

    def project(with_ssq, first=False):
        if first:
            h = _rms_rows(x_ref[...], g_ref[...]).astype(BF16)
            h_ref[...] = h
            ssq_ref[...] = jnp.zeros_like(ssq_ref)
        else:
            h = h_ref[...]
        part = None
        for c0 in range(0, w_ref.shape[1], SUB_N):
            y = jnp.dot(h, w_ref[:, c0:c0 + SUB_N], preferred_element_type=F32)
            z = 0.5 * y * (1.0 + lax.erf(y * np.float32(np.sqrt(0.5))))
            z_ref[:, c0:c0 + SUB_N] = z.astype(BF16)
            if with_ssq:
                zz = z * z
                for c in range(SUB_N // LANES):
                    sl = zz[:, c * LANES:(c + 1) * LANES]
                    part = sl if part is None else part + sl
        if with_ssq:
            ssq_ref[...] += part

    pl.when(j == 0)(functools.partial(project, False, first=True))
    pl.when(jnp.logical_and(j > 0, j < n_u))(functools.partial(project, False))
    pl.when(j >= n_u)(functools.partial(project, True))


def _sgu_in(x, gain, w_in, layer):
    t, d = x.shape
    n = w_in.shape[2]
    tm, tn = 512, 2048
    n_u = (n // 2) // tn
    kern = functools.partial(_sgu_in_kernel, n_u=n_u)
    return pl.pallas_call(
        kern,
        grid=(t // tm, n // tn),
        in_specs=[
            pl.BlockSpec((tm, d), lambda i, j: (i, 0)),
            pl.BlockSpec((1, d), lambda i, j: (0, 0)),
            pl.BlockSpec((None, d, tn), lambda i, j: (layer, 0, j)),
        ],
        out_specs=[
            pl.BlockSpec((tm, tn), lambda i, j: (i, j)),
            pl.BlockSpec((tm, LANES), lambda i, j: (i, 0)),
        ],
        out_shape=[
            jax.ShapeDtypeStruct((t, n), BF16),
            jax.ShapeDtypeStruct((t, LANES), F32),
        ],
        scratch_shapes=[pltpu.VMEM((tm, d), BF16)],
        compiler_params=_params("parallel", "arbitrary"),
        name="sgu_in",
    )(x, gain.reshape(1, d), w_in)


def _sgu_out_kernel(x_ref, u_ref, v_ref, ssq_ref, vg_ref, ws_ref, bs_ref, wo_ref, o_ref, *, tm):
    g = pl.program_id(1)

    @pl.when(g == 0)
    def _():
        o_ref[...] = x_ref[...]

    inv = lax.rsqrt(jnp.sum(ssq_ref[...], axis=1, keepdims=True) / SGU_DIM + RMS_EPS)
    row = lax.broadcasted_iota(jnp.int32, (SGU_CHUNK, SGU_CHUNK), 0)
    col = lax.broadcasted_iota(jnp.int32, (SGU_CHUNK, SGU_CHUNK), 1)
    gd = SGU_GROUP_DIM
    acc = None
    for gg in range(ws_ref.shape[0]):
        gcols = slice(gg * gd, (gg + 1) * gd)
        vn = (v_ref[:, gcols].astype(F32) * inv * vg_ref[:, gcols]).astype(BF16)
        w_s = jnp.where(row >= col, ws_ref[gg], 0.0).astype(BF16)
        bias = bs_ref[gg][:, :1]
        parts = []
        for c in range(tm // SGU_CHUNK):
            vc = vn[c * SGU_CHUNK:(c + 1) * SGU_CHUNK, :]
            parts.append(jnp.dot(w_s, vc, preferred_element_type=F32) + bias)
        sv = jnp.concatenate(parts, axis=0)
        p = (u_ref[:, gcols].astype(F32) * sv).astype(BF16)
        y = jnp.dot(p, wo_ref[gcols, :], preferred_element_type=F32)
        acc = y if acc is None else acc + y
    o_ref[...] += acc


def _sgu_out(x, z, ssq, v_gain, w_spatial, b_spatial, w_out, layer):
    t, d = x.shape
    tm = 512
    gps = SGU_GROUPS_PER_STEP
    gw = gps * SGU_GROUP_DIM
    n_steps = SGU_GROUPS // gps
    b_rep = jnp.broadcast_to(b_spatial[:, :, None], (SGU_GROUPS, SGU_CHUNK, LANES))
    kern = functools.partial(_sgu_out_kernel, tm=tm)
    return pl.pallas_call(
        kern,
        grid=(t // tm, n_steps),
        in_specs=[
            pl.BlockSpec((tm, d), lambda i, g: (i, 0)),
            pl.BlockSpec((tm, gw), lambda i, g: (i, g)),
            pl.BlockSpec((tm, gw), lambda i, g: (i, n_steps + g)),
            pl.BlockSpec((tm, LANES), lambda i, g: (i, 0)),
            pl.BlockSpec((1, gw), lambda i, g: (0, g)),
            pl.BlockSpec((gps, SGU_CHUNK, SGU_CHUNK), lambda i, g: (g, 0, 0)),
            pl.BlockSpec((gps, SGU_CHUNK, LANES), lambda i, g: (g, 0, 0)),
            pl.BlockSpec((None, gw, d), lambda i, g: (layer, g, 0)),
        ],
        out_specs=pl.BlockSpec((tm, d), lambda i, g: (i, 0)),
        out_shape=jax.ShapeDtypeStruct((t, d), F32),
        compiler_params=_params("parallel", "arbitrary"),
        name="sgu_out",
    )(x, z, z, ssq, v_gain.reshape(1, SGU_DIM), w_spatial, b_rep, w_out)


def kernel(x, ffn1_norm, ffn1_w_gate, ffn1_w_up, ffn1_w_down, mix_norm, ffn2_norm, ffn2_w_gate, ffn2_w_up, ffn2_w_down, attn_w_qkv, attn_q_gain, attn_k_gain, attn_w_out, pool_w_group, pool_scale, sgu_w_in, sgu_v_gain, sgu_w_spatial, sgu_b_spatial, sgu_w_out):
    batch, seq, d = x.shape
    assert d == D_MODEL and seq % MOBA_BLOCK == 0 and seq % SGU_CHUNK == 0
    xt = x.reshape(batch * seq, d)
    ffn1_f32 = (ffn1_w_gate, ffn1_w_up, ffn1_w_down)
    ffn2_f32 = (ffn2_w_gate, ffn2_w_up, ffn2_w_down)
    w_ffn = tuple(w[0].astype(BF16) for w in ffn1_f32)
    w_qkv, w_attn_out = attn_w_qkv.astype(BF16), attn_w_out.astype(BF16)
    w_sgu_in, w_sgu_out = sgu_w_in.astype(BF16), sgu_w_out.astype(BF16)
    for i in range(DEPTH):
        xt, w_ffn = _ffn(xt, ffn1_norm[i], w_ffn, ffn2_f32, i)
        kind, j = i % N_MIXERS, i // N_MIXERS
        if kind == 0:
            q, k, v = _qkv_proj(xt, mix_norm[i], w_qkv, j, attn_q_gain[j], attn_k_gain[j])
            o = _moba_attention(q, k, v, batch, seq)
            xt = _proj_residual(xt, o, w_attn_out, j)
        elif kind == 1:
            xt = _pool_mixer(xt, mix_norm[i], pool_w_group[j], pool_scale[j], batch, seq)
        else:
            z, ssq = _sgu_in(xt, mix_norm[i], w_sgu_in, j)
            xt = _sgu_out(xt, z, ssq, sgu_v_gain[j], sgu_w_spatial[j], sgu_b_spatial[j], w_sgu_out, j)
        if i + 1 < DEPTH:
            xt, w_ffn = _ffn(xt, ffn2_norm[i], w_ffn, ffn1_f32, i + 1)
        else:
            xt, _ = _ffn(xt, ffn2_norm[i], w_ffn)
    return xt.reshape(batch, seq, d)
```

```python
import functools

import jax
import jax.numpy as jnp
import numpy as np
from jax import lax
from jax.experimental import pallas as pl
from jax.experimental.pallas import tpu as pltpu

D_MODEL = 2048
DEPTH = 4
N_MIXERS = 3
RMS_EPS = 1e-6
D_FF = 5504
N_HEADS = 16
HEAD_DIM = D_MODEL // N_HEADS
MOBA_BLOCK = 256
MOBA_TOPK = 3
POOL_WINDOWS = (2, 4, 8, 16)
N_POOL_GROUPS = len(POOL_WINDOWS)
POOL_GROUP_DIM = D_MODEL // N_POOL_GROUPS
POOL_HALO = max(POOL_WINDOWS)
SGU_DIM = 3 * D_MODEL
SGU_CHUNK = 128
SGU_GROUPS = 8
SGU_GROUP_DIM = SGU_DIM // SGU_GROUPS

LANES = 128
SUBLANES = 8
MXU_DIM = 256
VMEM_LIMIT = 58 * 1024 * 1024

F32 = jnp.float32
BF16 = jnp.bfloat16
MASK_BIAS = -1e9
LOG2E = float(np.log2(np.e))
ATTN_GROUP = 4
ATTN_HEADS_PER_STEP = 2
ATTN_QBLOCKS_PER_STEP = 4
SGU_GROUPS_PER_STEP = 2
NT_DIMS = (((1,), (1,)), ((), ()))


SUB_N = 2 * MXU_DIM
FFN_TM = 1024
FFN_TF = 512


def _params(*sem):
    return pltpu.CompilerParams(dimension_semantics=sem, vmem_limit_bytes=VMEM_LIMIT)


def _rms_rows(x, gain):
    ms = jnp.mean(x * x, axis=-1, keepdims=True)
    return x * lax.rsqrt(ms + RMS_EPS) * gain


def _ffn_kernel(*refs, convert_next):
    if convert_next:
        x_ref, g_ref, wg_ref, wu_ref, wd_ref, *next_f32, o_ref, og_ref, ou_ref, od_ref, h_ref = refs
        for src, dst in zip(next_f32, (og_ref, ou_ref, od_ref)):
            dst[...] = src[...].astype(BF16)
    else:
        x_ref, g_ref, wg_ref, wu_ref, wd_ref, o_ref, h_ref = refs
    f = pl.program_id(1)
    tf = wd_ref.shape[0]

    def step(first):
        if first:
            x = x_ref[...]
            h = _rms_rows(x, g_ref[...]).astype(BF16)
            h_ref[...] = h
        else:
            h = h_ref[...]
        gate = jnp.dot(h, wg_ref[...], preferred_element_type=F32)
        up = jnp.dot(h, wu_ref[...], preferred_element_type=F32)
        a = (gate * jax.nn.sigmoid(gate)) * up * 0.5
        valid = D_FF - f * tf
        a = jnp.where(lax.broadcasted_iota(jnp.int32, (1, tf), 1) < valid, a, 0.0)
        wd = wd_ref[...]
        wd = jnp.where(lax.broadcasted_iota(jnp.int32, (tf, 1), 0) < valid, wd, jnp.zeros_like(wd))
        y = jnp.dot(a.astype(BF16), wd, preferred_element_type=F32)
        if first:
            o_ref[...] = x + y
        else:
            o_ref[...] += y

    pl.when(f == 0)(functools.partial(step, True))
    pl.when(f > 0)(functools.partial(step, False))


def _ffn(x, gain, weights, next_f32=None, next_layer=None):
    t, d = x.shape
    tm, tf = FFN_TM, FFN_TF
    n_tiles = t // tm
    in_specs = [
        pl.BlockSpec((tm, d), lambda i, f: (i, 0)),
        pl.BlockSpec((1, d), lambda i, f: (0, 0)),
        pl.BlockSpec((d, tf), lambda i, f: (0, f)),
        pl.BlockSpec((d, tf), lambda i, f: (0, f)),
        pl.BlockSpec((tf, d), lambda i, f: (f, 0)),
    ]
    out_specs = [pl.BlockSpec((tm, d), lambda i, f: (i, 0))]
    out_shape = [jax.ShapeDtypeStruct((t, d), F32)]
    operands = [x, gain.reshape(1, d), *weights]
    if next_f32 is not None:
        rows = d // n_tiles
        in_specs += [
            pl.BlockSpec((None, rows, tf), lambda i, f: (next_layer, i, f)),
            pl.BlockSpec((None, rows, tf), lambda i, f: (next_layer, i, f)),
            pl.BlockSpec((None, tf, rows), lambda i, f: (next_layer, f, i)),
        ]
        out_specs += [
            pl.BlockSpec((rows, tf), lambda i, f: (i, f)),
            pl.BlockSpec((rows, tf), lambda i, f: (i, f)),
            pl.BlockSpec((tf, rows), lambda i, f: (f, i)),
        ]
        out_shape += [jax.ShapeDtypeStruct(w.shape[1:], BF16) for w in next_f32]
        operands += list(next_f32)
    outs = pl.pallas_call(
        functools.partial(_ffn_kernel, convert_next=next_f32 is not None),
        grid=(n_tiles, pl.cdiv(D_FF, tf)),
        in_specs=in_specs,
        out_specs=out_specs,
        out_shape=out_shape,
        scratch_shapes=[pltpu.VMEM((tm, d), BF16)],
        compiler_params=_params("parallel", "arbitrary"),
        name="ffn",
    )(*operands)
    return outs[0], tuple(outs[1:])


def _qkv_kernel(x_ref, g_ref, w_ref, qg_ref, kg_ref, q_ref, k_ref, v_ref, h_ref):
    j = pl.program_id(1)

    def head_norm(y, gain):
        cols = []
        for hh in range(y.shape[1] // HEAD_DIM):
            cols.append(_rms_rows(y[:, hh * HEAD_DIM:(hh + 1) * HEAD_DIM], gain))
        return jnp.concatenate(cols, axis=1)

    def project(out_ref, epilogue, first=False):
        if first:
            h = _rms_rows(x_ref[...], g_ref[...]).astype(BF16)
            h_ref[...] = h
        else:
            h = h_ref[...]
        for c0 in range(0, w_ref.shape[1], SUB_N):
            y = jnp.dot(h, w_ref[:, c0:c0 + SUB_N], preferred_element_type=F32)
            out_ref[:, c0:c0 + SUB_N] = epilogue(y).astype(out_ref.dtype)

    @pl.when(j == 0)
    def _():
        project(q_ref, lambda y: head_norm(y, qg_ref[...]), first=True)

    @pl.when(j == 1)
    def _():
        project(k_ref, lambda y: head_norm(y, kg_ref[...]))

    @pl.when(j == 2)
    def _():
        project(v_ref, lambda y: y)


def _qkv_proj(x, gain, w, layer, q_gain, k_gain):
    t, d = x.shape
    tm = 512
    return pl.pallas_call(
        _qkv_kernel,
        grid=(t // tm, 3),
        in_specs=[
            pl.BlockSpec((tm, d), lambda i, j: (i, 0)),
            pl.BlockSpec((1, d), lambda i, j: (0, 0)),
            pl.BlockSpec((None, d, d), lambda i, j: (layer, 0, j)),
            pl.BlockSpec((1, HEAD_DIM), lambda i, j: (0, 0)),
            pl.BlockSpec((1, HEAD_DIM), lambda i, j: (0, 0)),
        ],
        out_specs=[
            pl.BlockSpec((tm, d), lambda i, j: (i, 0)),
            pl.BlockSpec((tm, d), lambda i, j: (i, 0)),
            pl.BlockSpec((tm, d), lambda i, j: (i, 0)),
        ],
        out_shape=[
            jax.ShapeDtypeStruct((t, d), F32),
            jax.ShapeDtypeStruct((t, d), BF16),
            jax.ShapeDtypeStruct((t, d), BF16),
        ],
        scratch_shapes=[pltpu.VMEM((tm, d), BF16)],
        compiler_params=_params("parallel", "arbitrary"),
        name="qkv_proj",
    )(x, gain.reshape(1, d), w, q_gain.reshape(1, HEAD_DIM), k_gain.reshape(1, HEAD_DIM))


def _attn_step(i, group, hh, out_rows, slope_ref, qa_ref, kt_ref, va_ref, o_ref):
    bs = MOBA_BLOCK
    c1 = HEAD_DIM ** -0.5 * LOG2E
    hcols = slice(hh * HEAD_DIM, (hh + 1) * HEAD_DIM)
    own = slice(i * bs, (i + 1) * bs)
    q_aug = qa_ref[hh, own, :]
    slope2 = slope_ref[hh] * LOG2E
    key_off = lax.broadcasted_iota(jnp.int32, (1, bs), 1)

    def col_bias(j):
        return slope2 * ((j - i) * bs + key_off).astype(F32)

    row = lax.broadcasted_iota(jnp.int32, (bs, bs), 0)
    col = lax.broadcasted_iota(jnp.int32, (bs, bs), 1)
    s = jnp.dot(q_aug[:, :HEAD_DIM], kt_ref[hh, :HEAD_DIM, own], preferred_element_type=F32)
    t_own = jnp.where(row >= col, s * c1 + col_bias(i), -jnp.inf)
    va_own = va_ref[hh, own, :]
    n_past = i

    def row_max(ts):
        tmax = functools.reduce(jnp.maximum, ts)
        return jnp.max(jnp.maximum(tmax[:, :LANES], tmax[:, LANES:]), axis=1, keepdims=True)

    m = acc = None
    n_groups = max(1, (n_past + group // 2) // group)
    bounds = [n_past * g // n_groups for g in range(n_groups + 1)]
    for j0, j1 in zip(bounds[:-1], bounds[1:]):
        ts = []
        if j1 > j0:
            s = jnp.dot(q_aug, kt_ref[hh, :, j0 * bs:j1 * bs], preferred_element_type=F32)
            ts = [s[:, (j - j0) * bs:(j - j0 + 1) * bs] * c1 + col_bias(j) for j in range(j0, j1)]
        if j0 == 0:
            m = row_max(ts + [t_own])
            acc = jnp.dot(jnp.exp2(t_own - m).astype(BF16), va_own, preferred_element_type=F32)
        else:
            m_new = jnp.maximum(m, row_max(ts))
            acc = jnp.exp2(m - m_new) * acc
            m = m_new
        if ts:
            pcat = jnp.concatenate([jnp.exp2(t - m).astype(BF16) for t in ts], axis=1)
            acc = acc + jnp.dot(pcat, va_ref[hh, j0 * bs:j1 * bs, :], preferred_element_type=F32)

    o_ref[out_rows, hcols] = (acc[:, :HEAD_DIM] / acc[:, HEAD_DIM:]).astype(o_ref.dtype)


def _attn_prepare_keys(hh, j, k_ref, v_ref, kt_ref, va_ref, kmean_ref):
    bs = MOBA_BLOCK
    hcols = slice(hh * HEAD_DIM, (hh + 1) * HEAD_DIM)
    rows = slice(j * bs, (j + 1) * bs)
    va_ref[hh, rows, :HEAD_DIM] = v_ref[rows, hcols]
    va_ref[hh, rows, HEAD_DIM:] = jnp.ones((bs, LANES), BF16)
    kj = k_ref[rows, hcols].astype(F32)
    kmean_ref[hh, j:j + 1, :] = jnp.mean(kj, axis=0, keepdims=True)
    kt_ref[hh, :HEAD_DIM, rows] = kj.T.astype(BF16)
    feat = lax.broadcasted_iota(jnp.int32, (LANES, bs), 0)
    kt_ref[hh, HEAD_DIM:, rows] = jnp.where(feat == j, 1.0, 0.0).astype(BF16)


def _attn_prepare_queries(hh, i, q_ref, qa_ref, kmean_ref):
    bs = MOBA_BLOCK
    hcols = slice(hh * HEAD_DIM, (hh + 1) * HEAD_DIM)
    rows = slice(i * bs, (i + 1) * bs)
    q32 = q_ref[rows, hcols]
    q_hi = q32.astype(BF16)
    qa_ref[hh, rows, :HEAD_DIM] = q_hi
    if i == 0:
        qa_ref[hh, rows, HEAD_DIM:] = jnp.zeros((bs, LANES), BF16)
        return
    n_rows = -(-i // SUBLANES) * SUBLANES
    q_lo = (q32 - q_hi.astype(F32)).astype(BF16)
    km = kmean_ref[hh, :n_rows, :]
    km_hi = km.astype(BF16)
    km_lo = (km - km_hi.astype(F32)).astype(BF16)

    def dot_nt(a, b):
        return lax.dot_general(a, b, NT_DIMS, preferred_element_type=F32)

    gate = dot_nt(km_hi, q_hi) + (dot_nt(km_hi, q_lo) + dot_nt(km_lo, q_hi))
    blk = lax.broadcasted_iota(jnp.int32, gate.shape, 0)
    rank = jnp.zeros(gate.shape, jnp.int32)
    for jp in range(i):
        gj = gate[jp:jp + 1, :]
        beats = jnp.logical_or(gj > gate, jnp.logical_and(gj == gate, blk > jp))
        rank = rank + jnp.where(beats, 1, 0)
    selected = jnp.logical_and(blk < i, rank < MOBA_TOPK)
    bias_t = jnp.where(selected, 0.0, MASK_BIAS).astype(F32)
    bias_t = jnp.concatenate([bias_t, jnp.full((LANES - n_rows, bs), MASK_BIAS, F32)], axis=0)
    qa_ref[hh, rows, HEAD_DIM:] = bias_t.T.astype(BF16)


def _attn_kernel(slope_ref, q_ref, k_ref, v_ref, o_ref, qa_ref, kt_ref, va_ref, kmean_ref, *, n_blocks,
                 group):
    bs = MOBA_BLOCK
    step = pl.program_id(2)
    heads = q_ref.shape[1] // HEAD_DIM
    q_blocks = o_ref.shape[0] // bs

    def prepare(blocks):
        for i in blocks:
            for hh in range(heads):
                _attn_prepare_keys(hh, i, k_ref, v_ref, kt_ref, va_ref, kmean_ref)
                _attn_prepare_queries(hh, i, q_ref, qa_ref, kmean_ref)

    for c in range(n_blocks // q_blocks):
        @pl.when(step == c)
        def _(c=c):
            first = c * q_blocks
            if c == 0:
                kmean_ref[...] = jnp.zeros_like(kmean_ref)
                prepare(range(q_blocks))
            for sub in range(q_blocks):
                for hh in range(heads):
                    _attn_step(first + sub, group, hh, slice(sub * bs, (sub + 1) * bs), slope_ref,
                               qa_ref, kt_ref, va_ref, o_ref)
            prepare(range(first + q_blocks, min(first + 2 * q_blocks, n_blocks)))


def _moba_attention(q, k, v, batch, seq):
    t, d = q.shape
    bs = MOBA_BLOCK
    n_blocks = seq // bs
    slopes = np.asarray(2.0 ** (-8.0 * np.arange(1, N_HEADS + 1) / N_HEADS), dtype=np.float32)
    slopes = jnp.asarray(np.broadcast_to(slopes[:, None, None], (N_HEADS, 1, bs)))
    kern = functools.partial(_attn_kernel, n_blocks=n_blocks, group=ATTN_GROUP)
    hps = ATTN_HEADS_PER_STEP
    hw = hps * HEAD_DIM
    qbs = ATTN_QBLOCKS_PER_STEP
    n_steps = n_blocks // qbs
    return pl.pallas_call(
        kern,
        grid=(batch, N_HEADS // hps, n_steps),
        in_specs=[
            pl.BlockSpec((hps, 1, bs), lambda b, h, i: (h, 0, 0)),
            pl.BlockSpec((seq, hw), lambda b, h, i: (b, h)),
            pl.BlockSpec((seq, hw), lambda b, h, i: (b, h)),
            pl.BlockSpec((seq, hw), lambda b, h, i: (b, h)),
        ],
        out_specs=pl.BlockSpec((qbs * bs, hw), lambda b, h, i: (b * n_steps + i, h)),
        out_shape=jax.ShapeDtypeStruct((t, d), BF16),
        scratch_shapes=[
            pltpu.VMEM((hps, seq, HEAD_DIM + LANES), BF16),
            pltpu.VMEM((hps, HEAD_DIM + LANES, seq), BF16),
            pltpu.VMEM((hps, seq, HEAD_DIM + LANES), BF16),
            pltpu.VMEM((hps, n_blocks, HEAD_DIM), F32),
        ],
        compiler_params=_params("parallel", "parallel", "arbitrary"),
        name="moba_attn",
    )(slopes, q, k, v)


def _proj_res_kernel(x_ref, a_ref, w_ref, o_ref):
    a = a_ref[...]
    for c0 in range(0, w_ref.shape[1], SUB_N):
        cols = slice(c0, c0 + SUB_N)
        o_ref[:, cols] = x_ref[:, cols] + jnp.dot(a, w_ref[:, cols], preferred_element_type=F32)


def _proj_residual(x, a, w, layer):
    t, n = x.shape
    kdim = a.shape[1]
    tm = 512
    return pl.pallas_call(
        _proj_res_kernel,
        grid=(t // tm,),
        in_specs=[
            pl.BlockSpec((tm, n), lambda i: (i, 0)),
            pl.BlockSpec((tm, kdim), lambda i: (i, 0)),
            pl.BlockSpec((None, kdim, n), lambda i: (layer, 0, 0)),
        ],
        out_specs=pl.BlockSpec((tm, n), lambda i: (i, 0)),
        out_shape=jax.ShapeDtypeStruct((t, n), F32),
        compiler_params=_params("parallel"),
        name="proj_residual",
    )(x, a, w)


def _pool_kernel(x_ref, halo_ref, g_ref, w_ref, sc_ref, o_ref, *, ts):
    i = pl.program_id(1)
    x = x_ref[...]
    gain = g_ref[...]
    h = _rms_rows(x, gain)
    h_halo = jnp.where(i == 0, 0.0, _rms_rows(halo_ref[...], gain))
    he = jnp.concatenate([h_halo, h], axis=0)
    pos = i * ts + lax.broadcasted_iota(jnp.int32, (ts, 1), 0)
    count = (pos + 1).astype(F32)
    cg = POOL_GROUP_DIM
    outs = []
    for g, win in enumerate(POOL_WINDOWS):
        a = he[:, g * cg:(g + 1) * cg]
        lead = 0
        width = 1
        while width < win:
            a = a[width:, :] + a[:-width, :]
            lead += width
            width *= 2
        start = POOL_HALO - lead
        pooled = a[start:start + ts, :] / jnp.minimum(count, float(win)) - h[:, g * cg:(g + 1) * cg]
        outs.append(jnp.dot(pooled.astype(BF16), w_ref[g], preferred_element_type=F32))
    y = jnp.concatenate(outs, axis=1)
    o_ref[...] = x + y * sc_ref[...]


def _pool_mixer(x, gain, w_group, scale, batch, seq):
    t, d = x.shape
    ts = 512
    per_seq = seq // ts
    halo_per_tile = ts // POOL_HALO
    kern = functools.partial(_pool_kernel, ts=ts)

    def halo_map(b, i):
        return (jnp.maximum((b * per_seq + i) * halo_per_tile - 1, 0), 0)

    return pl.pallas_call(
        kern,
        grid=(batch, per_seq),
        in_specs=[
            pl.BlockSpec((ts, d), lambda b, i: (b * per_seq + i, 0)),
            pl.BlockSpec((POOL_HALO, d), halo_map),
            pl.BlockSpec((1, d), lambda b, i: (0, 0)),
            pl.BlockSpec((N_POOL_GROUPS, POOL_GROUP_DIM, POOL_GROUP_DIM), lambda b, i: (0, 0, 0)),
            pl.BlockSpec((1, d), lambda b, i: (0, 0)),
        ],
        out_specs=pl.BlockSpec((ts, d), lambda b, i: (b * per_seq + i, 0)),
        out_shape=jax.ShapeDtypeStruct((t, d), F32),
        compiler_params=_params("parallel", "parallel"),
        name="pool_mixer",
    )(x, x, gain.reshape(1, d), w_group.astype(BF16), scale.reshape(1, d))


def _sgu_in_kernel(x_ref, g_ref, w_ref, z_ref, ssq_ref, h_ref, *, n_u):
    j = pl.program_id(1)

    def project(with_ssq, first=False):
        if first:
            h = _rms_rows(x_ref[...], g_ref[...]).astype(BF16)
            h_ref[...] = h
            ssq_ref[...] = jnp.zeros_like(ssq_ref)
        else:
            h = h_ref[...]
        part = None
        for c0 in range(0, w_ref.shape[1], SUB_N):
            y = jnp.dot(h, w_ref[:, c0:c0 + SUB_N], preferred_element_type=F32)
            z = 0.5 * y * (1.0 + lax.erf(y * np.float32(np.sqrt(0.5))))
            z_ref[:, c0:c0 + SUB_N] = z.astype(BF16)
            if with_ssq:
                zz = z * z
                for c in range(SUB_N // LANES):
                    sl = zz[:, c * LANES:(c + 1) * LANES]
                    part = sl if part is None else part + sl
        if with_ssq:
            ssq_ref[...] += part

    pl.when(j == 0)(functools.partial(project, False, first=True))
    pl.when(jnp.logical_and(j > 0, j < n_u))(functools.partial(project, False))
    pl.when(j >= n_u)(functools.partial(project, True))


def _sgu_in(x, gain, w_in, layer):
    t, d = x.shape
    n = w_in.shape[2]
    tm, tn = 1024, 2048
    n_u = (n // 2) // tn
    kern = functools.partial(_sgu_in_kernel, n_u=n_u)
    return pl.pallas_call(
        kern,
        grid=(t // tm, n // tn),
        in_specs=[
            pl.BlockSpec((tm, d), lambda i, j: (i, 0)),
            pl.BlockSpec((1, d), lambda i, j: (0, 0)),
            pl.BlockSpec((None, d, tn), lambda i, j: (layer, 0, j)),
        ],
        out_specs=[
            pl.BlockSpec((tm, tn), lambda i, j: (i, j)),
            pl.BlockSpec((tm, LANES), lambda i, j: (i, 0)),
        ],
        out_shape=[
            jax.ShapeDtypeStruct((t, n), BF16),
            jax.ShapeDtypeStruct((t, LANES), F32),
        ],
        scratch_shapes=[pltpu.VMEM((tm, d), BF16)],
        compiler_params=_params("parallel", "arbitrary"),
        name="sgu_in",
    )(x, gain.reshape(1, d), w_in)


def _sgu_out_kernel(x_ref, u_ref, v_ref, ssq_ref, vg_ref, ws_ref, bs_ref, wo_ref, o_ref, *, tm):
    g = pl.program_id(1)

    @pl.when(g == 0)
    def _():
        o_ref[...] = x_ref[...]

    inv = lax.rsqrt(jnp.sum(ssq_ref[...], axis=1, keepdims=True) / SGU_DIM + RMS_EPS)
    row = lax.broadcasted_iota(jnp.int32, (SGU_CHUNK, SGU_CHUNK), 0)
    col = lax.broadcasted_iota(jnp.int32, (SGU_CHUNK, SGU_CHUNK), 1)
    gd = SGU_GROUP_DIM
    acc = None
    for gg in range(ws_ref.shape[0]):
        gcols = slice(gg * gd, (gg + 1) * gd)
        vn = (v_ref[:, gcols].astype(F32) * inv * vg_ref[:, gcols]).astype(BF16)
        w_s = jnp.where(row >= col, ws_ref[gg], 0.0).astype(BF16)
        bias = bs_ref[gg][:, :1]
        parts = []
        for c in range(tm // SGU_CHUNK):
            vc = vn[c * SGU_CHUNK:(c + 1) * SGU_CHUNK, :]
            parts.append(jnp.dot(w_s, vc, preferred_element_type=F32) + bias)
        sv = jnp.concatenate(parts, axis=0)
        p = (u_ref[:, gcols].astype(F32) * sv).astype(BF16)
        y = jnp.dot(p, wo_ref[gcols, :], preferred_element_type=F32)
        acc = y if acc is None else acc + y
    o_ref[...] += acc


def _sgu_out(x, z, ssq, v_gain, w_spatial, b_spatial, w_out, layer):
    t, d = x.shape
    tm = 512
    gps = SGU_GROUPS_PER_STEP
    gw = gps * SGU_GROUP_DIM
    n_steps = SGU_GROUPS // gps
    b_rep = jnp.broadcast_to(b_spatial[:, :, None], (SGU_GROUPS, SGU_CHUNK, LANES))
    kern = functools.partial(_sgu_out_kernel, tm=tm)
    return pl.pallas_call(
        kern,
        grid=(t // tm, n_steps),
        in_specs=[
            pl.BlockSpec((tm, d), lambda i, g: (i, 0)),
            pl.BlockSpec((tm, gw), lambda i, g: (i, g)),
            pl.BlockSpec((tm, gw), lambda i, g: (i, n_steps + g)),
            pl.BlockSpec((tm, LANES), lambda i, g: (i, 0)),
            pl.BlockSpec((1, gw), lambda i, g: (0, g)),
            pl.BlockSpec((gps, SGU_CHUNK, SGU_CHUNK), lambda i, g: (g, 0, 0)),
            pl.BlockSpec((gps, SGU_CHUNK, LANES), lambda i, g: (g, 0, 0)),
            pl.BlockSpec((None, gw, d), lambda i, g: (layer, g, 0)),
        ],
        out_specs=pl.BlockSpec((tm, d), lambda i, g: (i, 0)),
        out_shape=jax.ShapeDtypeStruct((t, d), F32),
        compiler_params=_params("parallel", "arbitrary"),
        name="sgu_out",
    )(x, z, z, ssq, v_gain.reshape(1, SGU_DIM), w_spatial, b_rep, w_out)


def kernel(x, ffn1_norm, ffn1_w_gate, ffn1_w_up, ffn1_w_down, mix_norm, ffn2_norm, ffn2_w_gate, ffn2_w_up, ffn2_w_down, attn_w_qkv, attn_q_gain, attn_k_gain, attn_w_out, pool_w_group, pool_scale, sgu_w_in, sgu_v_gain, sgu_w_spatial, sgu_b_spatial, sgu_w_out):
    batch, seq, d = x.shape
    assert d == D_MODEL and seq % MOBA_BLOCK == 0 and seq % SGU_CHUNK == 0
    xt = x.reshape(batch * seq, d)
    ffn1_f32 = (ffn1_w_gate, ffn1_w_up, ffn1_w_down)
    ffn2_f32 = (ffn2_w_gate, ffn2_w_up, ffn2_w_down)
    w_ffn = tuple(w[0].astype(BF16) for w in ffn1_f32)
    w_qkv, w_attn_out = attn_w_qkv.astype(BF16), attn_w_out.astype(BF16)
    w_sgu_in, w_sgu_out = sgu_w_in.astype(BF16), sgu_w_out.astype(BF16)
    for i in range(DEPTH):
        xt, w_ffn = _ffn(xt, ffn1_norm[i], w_ffn, ffn2_f32, i)
        kind, j = i % N_MIXERS, i // N_MIXERS
        if kind == 0:
            q, k, v = _qkv_proj(xt, mix_norm[i], w_qkv, j, attn_q_gain[j], attn_k_gain[j])
            o = _moba_attention(q, k, v, batch, seq)
            xt = _proj_residual(xt, o, w_attn_out, j)
        elif kind == 1:
            xt = _pool_mixer(xt, mix_norm[i], pool_w_group[j], pool_scale[j], batch, seq)
        else:
            z, ssq = _sgu_in(xt, mix_norm[i], w_sgu_in, j)
            xt = _sgu_out(xt, z, ssq, sgu_v_gain[j], sgu_w_spatial[j], sgu_b_spatial[j], w_sgu_out, j)
        if i + 1 < DEPTH:
            xt, w_ffn = _ffn(xt, ffn2_norm[i], w_ffn, ffn1_f32, i + 1)
        else:
            xt, _ = _ffn(xt, ffn2_norm[i], w_ffn)
    return xt.reshape(batch, seq, d)
```

```python
import functools

import jax
import jax.numpy as jnp
import numpy as np
from jax import lax
from jax.experimental import pallas as pl
from jax.experimental.pallas import tpu as pltpu

D_MODEL = 2048
DEPTH = 4
N_MIXERS = 3
RMS_EPS = 1e-6
D_FF = 5504
N_HEADS = 16
HEAD_DIM = D_MODEL // N_HEADS
MOBA_BLOCK = 256
MOBA_TOPK = 3
POOL_WINDOWS = (2, 4, 8, 16)
N_POOL_GROUPS = len(POOL_WINDOWS)
POOL_GROUP_DIM = D_MODEL // N_POOL_GROUPS
POOL_HALO = max(POOL_WINDOWS)
SGU_DIM = 3 * D_MODEL
SGU_CHUNK = 128
SGU_GROUPS = 8
SGU_GROUP_DIM = SGU_DIM // SGU_GROUPS

LANES = 128
SUBLANES = 8
MXU_DIM = 256
VMEM_LIMIT = 58 * 1024 * 1024

F32 = jnp.float32
BF16 = jnp.bfloat16
MASK_BIAS = -1e9
LOG2E = float(np.log2(np.e))
ATTN_GROUP = 4
ATTN_HEADS_PER_STEP = 2
ATTN_QBLOCKS_PER_STEP = 8
SGU_GROUPS_PER_STEP = 2
NT_DIMS = (((1,), (1,)), ((), ()))


SUB_N = 2 * MXU_DIM
FFN_TM = 1024
FFN_TF = 512


def _params(*sem):
    return pltpu.CompilerParams(dimension_semantics=sem, vmem_limit_bytes=VMEM_LIMIT)


def _rms_rows(x, gain):
    ms = jnp.mean(x * x, axis=-1, keepdims=True)
    return x * lax.rsqrt(ms + RMS_EPS) * gain


def _ffn_kernel(*refs, convert_next):
    if convert_next:
        x_ref, g_ref, wg_ref, wu_ref, wd_ref, *next_f32, o_ref, og_ref, ou_ref, od_ref, h_ref = refs
        for src, dst in zip(next_f32, (og_ref, ou_ref, od_ref)):
            dst[...] = src[...].astype(BF16)
    else:
        x_ref, g_ref, wg_ref, wu_ref, wd_ref, o_ref, h_ref = refs
    f = pl.program_id(1)
    tf = wd_ref.shape[0]

    def step(first):
        if first:
            x = x_ref[...]
            h = _rms_rows(x, g_ref[...]).astype(BF16)
            h_ref[...] = h
        else:
            h = h_ref[...]
        gate = jnp.dot(h, wg_ref[...], preferred_element_type=F32)
        up = jnp.dot(h, wu_ref[...], preferred_element_type=F32)
        a = (gate * jax.nn.sigmoid(gate)) * up * 0.5
        valid = D_FF - f * tf
        a = jnp.where(lax.broadcasted_iota(jnp.int32, (1, tf), 1) < valid, a, 0.0)
        wd = wd_ref[...]
        wd = jnp.where(lax.broadcasted_iota(jnp.int32, (tf, 1), 0) < valid, wd, jnp.zeros_like(wd))
        y = jnp.dot(a.astype(BF16), wd, preferred_element_type=F32)
        if first:
            o_ref[...] = x + y
        else:
            o_ref[...] += y

    pl.when(f == 0)(functools.partial(step, True))
    pl.when(f > 0)(functools.partial(step, False))


def _ffn(x, gain, weights, next_f32=None, next_layer=None):
    t, d = x.shape
    tm, tf = FFN_TM, FFN_TF
    n_tiles = t // tm
    in_specs = [
        pl.BlockSpec((tm, d), lambda i, f: (i, 0)),
        pl.BlockSpec((1, d), lambda i, f: (0, 0)),
        pl.BlockSpec((d, tf), lambda i, f: (0, f)),
        pl.BlockSpec((d, tf), lambda i, f: (0, f)),
        pl.BlockSpec((tf, d), lambda i, f: (f, 0)),
    ]
    out_specs = [pl.BlockSpec((tm, d), lambda i, f: (i, 0))]
    out_shape = [jax.ShapeDtypeStruct((t, d), F32)]
    operands = [x, gain.reshape(1, d), *weights]
    if next_f32 is not None:
        rows = d // n_tiles
        in_specs += [
            pl.BlockSpec((None, rows, tf), lambda i, f: (next_layer, i, f)),
            pl.BlockSpec((None, rows, tf), lambda i, f: (next_layer, i, f)),
            pl.BlockSpec((None, tf, rows), lambda i, f: (next_layer, f, i)),
        ]
        out_specs += [
            pl.BlockSpec((rows, tf), lambda i, f: (i, f)),
            pl.BlockSpec((rows, tf), lambda i, f: (i, f)),
            pl.BlockSpec((tf, rows), lambda i, f: (f, i)),
        ]
        out_shape += [jax.ShapeDtypeStruct(w.shape[1:], BF16) for w in next_f32]
        operands += list(next_f32)
    outs = pl.pallas_call(
        functools.partial(_ffn_kernel, convert_next=next_f32 is not None),
        grid=(n_tiles, pl.cdiv(D_FF, tf)),
        in_specs=in_specs,
        out_specs=out_specs,
        out_shape=out_shape,
        scratch_shapes=[pltpu.VMEM((tm, d), BF16)],
        compiler_params=_params("parallel", "arbitrary"),
        name="ffn",
    )(*operands)
    return outs[0], tuple(outs[1:])


def _qkv_kernel(x_ref, g_ref, w_ref, qg_ref, kg_ref, q_ref, k_ref, v_ref, h_ref):
    j = pl.program_id(1)

    def head_norm(y, gain):
        cols = []
        for hh in range(y.shape[1] // HEAD_DIM):
            cols.append(_rms_rows(y[:, hh * HEAD_DIM:(hh + 1) * HEAD_DIM], gain))
        return jnp.concatenate(cols, axis=1)

    def project(out_ref, epilogue, first=False):
        if first:
            h = _rms_rows(x_ref[...], g_ref[...]).astype(BF16)
            h_ref[...] = h
        else:
            h = h_ref[...]
        for c0 in range(0, w_ref.shape[1], SUB_N):
            y = jnp.dot(h, w_ref[:, c0:c0 + SUB_N], preferred_element_type=F32)
            out_ref[:, c0:c0 + SUB_N] = epilogue(y).astype(out_ref.dtype)

    @pl.when(j == 0)
    def _():
        project(q_ref, lambda y: head_norm(y, qg_ref[...]), first=True)

    @pl.when(j == 1)
    def _():
        project(k_ref, lambda y: head_norm(y, kg_ref[...]))

    @pl.when(j == 2)
    def _():
        project(v_ref, lambda y: y)


def _qkv_proj(x, gain, w, layer, q_gain, k_gain):
    t, d = x.shape
    tm = 512
    return pl.pallas_call(
        _qkv_kernel,
        grid=(t // tm, 3),
        in_specs=[
            pl.BlockSpec((tm, d), lambda i, j: (i, 0)),
            pl.BlockSpec((1, d), lambda i, j: (0, 0)),
            pl.BlockSpec((None, d, d), lambda i, j: (layer, 0, j)),
            pl.BlockSpec((1, HEAD_DIM), lambda i, j: (0, 0)),
            pl.BlockSpec((1, HEAD_DIM), lambda i, j: (0, 0)),
        ],
        out_specs=[
            pl.BlockSpec((tm, d), lambda i, j: (i, 0)),
            pl.BlockSpec((tm, d), lambda i, j: (i, 0)),
            pl.BlockSpec((tm, d), lambda i, j: (i, 0)),
        ],
        out_shape=[
            jax.ShapeDtypeStruct((t, d), F32),
            jax.ShapeDtypeStruct((t, d), BF16),
            jax.ShapeDtypeStruct((t, d), BF16),
        ],
        scratch_shapes=[pltpu.VMEM((tm, d), BF16)],
        compiler_params=_params("parallel", "arbitrary"),
        name="qkv_proj",
    )(x, gain.reshape(1, d), w, q_gain.reshape(1, HEAD_DIM), k_gain.reshape(1, HEAD_DIM))


def _attn_step(i, group, hh, out_rows, slope_ref, qa_ref, kt_ref, va_ref, o_ref):
    bs = MOBA_BLOCK
    c1 = HEAD_DIM ** -0.5 * LOG2E
    hcols = slice(hh * HEAD_DIM, (hh + 1) * HEAD_DIM)
    own = slice(i * bs, (i + 1) * bs)
    q_aug = qa_ref[hh, own, :]
    slope2 = slope_ref[hh] * LOG2E
    key_off = lax.broadcasted_iota(jnp.int32, (1, bs), 1)

    def col_bias(j):
        return slope2 * ((j - i) * bs + key_off).astype(F32)

    row = lax.broadcasted_iota(jnp.int32, (bs, bs), 0)
    col = lax.broadcasted_iota(jnp.int32, (bs, bs), 1)
    s = jnp.dot(q_aug[:, :HEAD_DIM], kt_ref[hh, :HEAD_DIM, own], preferred_element_type=F32)
    t_own = jnp.where(row >= col, s * c1 + col_bias(i), -jnp.inf)
    va_own = va_ref[hh, own, :]
    n_past = i

    def row_max(ts):
        tmax = functools.reduce(jnp.maximum, ts)
        return jnp.max(jnp.maximum(tmax[:, :LANES], tmax[:, LANES:]), axis=1, keepdims=True)

    m = acc = None
    n_groups = max(1, (n_past + group // 2) // group)
    bounds = [n_past * g // n_groups for g in range(n_groups + 1)]
    for j0, j1 in zip(bounds[:-1], bounds[1:]):
        ts = []
        if j1 > j0:
            s = jnp.dot(q_aug, kt_ref[hh, :, j0 * bs:j1 * bs], preferred_element_type=F32)
            ts = [s[:, (j - j0) * bs:(j - j0 + 1) * bs] * c1 + col_bias(j) for j in range(j0, j1)]
        if j0 == 0:
            m = row_max(ts + [t_own])
            acc = jnp.dot(jnp.exp2(t_own - m).astype(BF16), va_own, preferred_element_type=F32)
        else:
            m_new = jnp.maximum(m, row_max(ts))
            acc = jnp.exp2(m - m_new) * acc
            m = m_new
        if ts:
            pcat = jnp.concatenate([jnp.exp2(t - m).astype(BF16) for t in ts], axis=1)
            acc = acc + jnp.dot(pcat, va_ref[hh, j0 * bs:j1 * bs, :], preferred_element_type=F32)

    o_ref[out_rows, hcols] = (acc[:, :HEAD_DIM] / acc[:, HEAD_DIM:]).astype(o_ref.dtype)


def _attn_prepare_keys(hh, j, k_ref, v_ref, kt_ref, va_ref, kmean_ref):
    bs = MOBA_BLOCK
    hcols = slice(hh * HEAD_DIM, (hh + 1) * HEAD_DIM)
    rows = slice(j * bs, (j + 1) * bs)
    va_ref[hh, rows, :HEAD_DIM] = v_ref[rows, hcols]
    va_ref[hh, rows, HEAD_DIM:] = jnp.ones((bs, LANES), BF16)
    kj = k_ref[rows, hcols].astype(F32)
    kmean_ref[hh, j:j + 1, :] = jnp.mean(kj, axis=0, keepdims=True)
    kt_ref[hh, :HEAD_DIM, rows] = kj.T.astype(BF16)
    feat = lax.broadcasted_iota(jnp.int32, (LANES, bs), 0)
    kt_ref[hh, HEAD_DIM:, rows] = jnp.where(feat == j, 1.0, 0.0).astype(BF16)


def _attn_prepare_queries(hh, i, q_ref, qa_ref, kmean_ref):
    bs = MOBA_BLOCK
    hcols = slice(hh * HEAD_DIM, (hh + 1) * HEAD_DIM)
    rows = slice(i * bs, (i + 1) * bs)
    q32 = q_ref[rows, hcols]
    q_hi = q32.astype(BF16)
    qa_ref[hh, rows, :HEAD_DIM] = q_hi
    if i == 0:
        qa_ref[hh, rows, HEAD_DIM:] = jnp.zeros((bs, LANES), BF16)
        return
    n_rows = -(-i // SUBLANES) * SUBLANES
    q_lo = (q32 - q_hi.astype(F32)).astype(BF16)
    km = kmean_ref[hh, :n_rows, :]
    km_hi = km.astype(BF16)
    km_lo = (km - km_hi.astype(F32)).astype(BF16)

    def dot_nt(a, b):
        return lax.dot_general(a, b, NT_DIMS, preferred_element_type=F32)

    gate = dot_nt(km_hi, q_hi) + (dot_nt(km_hi, q_lo) + dot_nt(km_lo, q_hi))
    blk = lax.broadcasted_iota(jnp.int32, gate.shape, 0)
    rank = jnp.zeros(gate.shape, jnp.int32)
    for jp in range(i):
        gj = gate[jp:jp + 1, :]
        beats = jnp.logical_or(gj > gate, jnp.logical_and(gj == gate, blk > jp))
        rank = rank + jnp.where(beats, 1, 0)
    selected = jnp.logical_and(blk < i, rank < MOBA_TOPK)
    bias_t = jnp.where(selected, 0.0, MASK_BIAS).astype(F32)
    bias_t = jnp.concatenate([bias_t, jnp.full((LANES - n_rows, bs), MASK_BIAS, F32)], axis=0)
    qa_ref[hh, rows, HEAD_DIM:] = bias_t.T.astype(BF16)


def _attn_kernel(slope_ref, q_ref, k_ref, v_ref, o_ref, qa_ref, kt_ref, va_ref, kmean_ref, *, n_blocks,
                 group):
    bs = MOBA_BLOCK
    step = pl.program_id(2)
    heads = q_ref.shape[1] // HEAD_DIM
    q_blocks = o_ref.shape[0] // bs

    def prepare(blocks):
        for i in blocks:
            for hh in range(heads):
                _attn_prepare_keys(hh, i, k_ref, v_ref, kt_ref, va_ref, kmean_ref)
                _attn_prepare_queries(hh, i, q_ref, qa_ref, kmean_ref)

    for c in range(n_blocks // q_blocks):
        @pl.when(step == c)
        def _(c=c):
            first = c * q_blocks
            if c == 0:
                kmean_ref[...] = jnp.zeros_like(kmean_ref)
                prepare(range(q_blocks))
            for sub in range(q_blocks):
                for hh in range(heads):
                    _attn_step(first + sub, group, hh, slice(sub * bs, (sub + 1) * bs), slope_ref,
                               qa_ref, kt_ref, va_ref, o_ref)
            prepare(range(first + q_blocks, min(first + 2 * q_blocks, n_blocks)))


def _moba_attention(q, k, v, batch, seq):
    t, d = q.shape
    bs = MOBA_BLOCK
    n_blocks = seq // bs
    slopes = np.asarray(2.0 ** (-8.0 * np.arange(1, N_HEADS + 1) / N_HEADS), dtype=np.float32)
    slopes = jnp.asarray(np.broadcast_to(slopes[:, None, None], (N_HEADS, 1, bs)))
    kern = functools.partial(_attn_kernel, n_blocks=n_blocks, group=ATTN_GROUP)
    hps = ATTN_HEADS_PER_STEP
    hw = hps * HEAD_DIM
    qbs = ATTN_QBLOCKS_PER_STEP
    n_steps = n_blocks // qbs
    return pl.pallas_call(
        kern,
        grid=(batch, N_HEADS // hps, n_steps),
        in_specs=[
            pl.BlockSpec((hps, 1, bs), lambda b, h, i: (h, 0, 0)),
            pl.BlockSpec((seq, hw), lambda b, h, i: (b, h)),
            pl.BlockSpec((seq, hw), lambda b, h, i: (b, h)),
            pl.BlockSpec((seq, hw), lambda b, h, i: (b, h)),
        ],
        out_specs=pl.BlockSpec((qbs * bs, hw), lambda b, h, i: (b * n_steps + i, h)),
        out_shape=jax.ShapeDtypeStruct((t, d), BF16),
        scratch_shapes=[
            pltpu.VMEM((hps, seq, HEAD_DIM + LANES), BF16),
            pltpu.VMEM((hps, HEAD_DIM + LANES, seq), BF16),
            pltpu.VMEM((hps, seq, HEAD_DIM + LANES), BF16),
            pltpu.VMEM((hps, n_blocks, HEAD_DIM), F32),
        ],
        compiler_params=_params("parallel", "parallel", "arbitrary"),
        name="moba_attn",
    )(slopes, q, k, v)


def _proj_res_kernel(x_ref, a_ref, w_ref, o_ref):
    a = a_ref[...]
    for c0 in range(0, w_ref.shape[1], SUB_N):
        cols = slice(c0, c0 + SUB_N)
        o_ref[:, cols] = x_ref[:, cols] + jnp.dot(a, w_ref[:, cols], preferred_element_type=F32)


def _proj_residual(x, a, w, layer):
    t, n = x.shape
    kdim = a.shape[1]
    tm = 512
    return pl.pallas_call(
        _proj_res_kernel,
        grid=(t // tm,),
        in_specs=[
            pl.BlockSpec((tm, n), lambda i: (i, 0)),
            pl.BlockSpec((tm, kdim), lambda i: (i, 0)),
            pl.BlockSpec((None, kdim, n), lambda i: (layer, 0, 0)),
        ],
        out_specs=pl.BlockSpec((tm, n), lambda i: (i, 0)),
        out_shape=jax.ShapeDtypeStruct((t, n), F32),
        compiler_params=_params("parallel"),
        name="proj_residual",
    )(x, a, w)


def _pool_kernel(x_ref, halo_ref, g_ref, w_ref, sc_ref, o_ref, *, ts):
    i = pl.program_id(1)
    x = x_ref[...]
    gain = g_ref[...]
    h = _rms_rows(x, gain)
    h_halo = jnp.where(i == 0, 0.0, _rms_rows(halo_ref[...], gain))
    he = jnp.concatenate([h_halo, h], axis=0)
    pos = i * ts + lax.broadcasted_iota(jnp.int32, (ts, 1), 0)
    count = (pos + 1).astype(F32)
    cg = POOL_GROUP_DIM
    outs = []
    for g, win in enumerate(POOL_WINDOWS):
        a = he[:, g * cg:(g + 1) * cg]
        lead = 0
        width = 1
        while width < win:
            a = a[width:, :] + a[:-width, :]
            lead += width
            width *= 2
        start = POOL_HALO - lead
        pooled = a[start:start + ts, :] / jnp.minimum(count, float(win)) - h[:, g * cg:(g + 1) * cg]
        outs.append(jnp.dot(pooled.astype(BF16), w_ref[g], preferred_element_type=F32))
    y = jnp.concatenate(outs, axis=1)
    o_ref[...] = x + y * sc_ref[...]


def _pool_mixer(x, gain, w_group, scale, batch, seq):
    t, d = x.shape
    ts = 512
    per_seq = seq // ts
    halo_per_tile = ts // POOL_HALO
    kern = functools.partial(_pool_kernel, ts=ts)

    def halo_map(b, i):
        return (jnp.maximum((b * per_seq + i) * halo_per_tile - 1, 0), 0)

    return pl.pallas_call(
        kern,
        grid=(batch, per_seq),
        in_specs=[
            pl.BlockSpec((ts, d), lambda b, i: (b * per_seq + i, 0)),
            pl.BlockSpec((POOL_HALO, d), halo_map),
            pl.BlockSpec((1, d), lambda b, i: (0, 0)),
            pl.BlockSpec((N_POOL_GROUPS, POOL_GROUP_DIM, POOL_GROUP_DIM), lambda b, i: (0, 0, 0)),
            pl.BlockSpec((1, d), lambda b, i: (0, 0)),
        ],
        out_specs=pl.BlockSpec((ts, d), lambda b, i: (b * per_seq + i, 0)),
        out_shape=jax.ShapeDtypeStruct((t, d), F32),
        compiler_params=_params("parallel", "parallel"),
        name="pool_mixer",
    )(x, x, gain.reshape(1, d), w_group.astype(BF16), scale.reshape(1, d))


def _sgu_in_kernel(x_ref, g_ref, w_ref, z_ref, ssq_ref, h_ref, *, n_u):
    j = pl.program_id(1)

    def project(with_ssq, first=False):
        if first:
            h = _rms_rows(x_ref[...], g_ref[...]).astype(BF16)
            h_ref[...] = h
            ssq_ref[...] = jnp.zeros_like(ssq_ref)
        else:
            h = h_ref[...]
        part = None
        for c0 in range(0, w_ref.shape[1], SUB_N):
            y = jnp.dot(h, w_ref[:, c0:c0 + SUB_N], preferred_element_type=F32)
            z = 0.5 * y * (1.0 + lax.erf(y * np.float32(np.sqrt(0.5))))
            z_ref[:, c0:c0 + SUB_N] = z.astype(BF16)
            if with_ssq:
                zz = z * z
                for c in range(SUB_N // LANES):
                    sl = zz[:, c * LANES:(c + 1) * LANES]
                    part = sl if part is None else part + sl
        if with_ssq:
            ssq_ref[...] += part

    pl.when(j == 0)(functools.partial(project, False, first=True))
    pl.when(jnp.logical_and(j > 0, j < n_u))(functools.partial(project, False))
    pl.when(j >= n_u)(functools.partial(project, True))


def _sgu_in(x, gain, w_in, layer):
    t, d = x.shape
    n = w_in.shape[2]
    tm, tn = 1024, 2048
    n_u = (n // 2) // tn
    kern = functools.partial(_sgu_in_kernel, n_u=n_u)
    return pl.pallas_call(
        kern,
        grid=(t // tm, n // tn),
        in_specs=[
            pl.BlockSpec((tm, d), lambda i, j: (i, 0)),
            pl.BlockSpec((1, d), lambda i, j: (0, 0)),
            pl.BlockSpec((None, d, tn), lambda i, j: (layer, 0, j)),
        ],
        out_specs=[
            pl.BlockSpec((tm, tn), lambda i, j: (i, j)),
            pl.BlockSpec((tm, LANES), lambda i, j: (i, 0)),
        ],
        out_shape=[
            jax.ShapeDtypeStruct((t, n), BF16),
            jax.ShapeDtypeStruct((t, LANES), F32),
        ],
        scratch_shapes=[pltpu.VMEM((tm, d), BF16)],
        compiler_params=_params("parallel", "arbitrary"),
        name="sgu_in",
    )(x, gain.reshape(1, d), w_in)


def _sgu_out_kernel(x_ref, u_ref, v_ref, ssq_ref, vg_ref, ws_ref, bs_ref, wo_ref, o_ref, *, tm):
    g = pl.program_id(1)

    @pl.when(g == 0)
    def _():
        o_ref[...] = x_ref[...]

    inv = lax.rsqrt(jnp.sum(ssq_ref[...], axis=1, keepdims=True) / SGU_DIM + RMS_EPS)
    row = lax.broadcasted_iota(jnp.int32, (SGU_CHUNK, SGU_CHUNK), 0)
    col = lax.broadcasted_iota(jnp.int32, (SGU_CHUNK, SGU_CHUNK), 1)
    gd = SGU_GROUP_DIM
    acc = None
    for gg in range(ws_ref.shape[0]):
        gcols = slice(gg * gd, (gg + 1) * gd)
        vn = (v_ref[:, gcols].astype(F32) * inv * vg_ref[:, gcols]).astype(BF16)
        w_s = jnp.where(row >= col, ws_ref[gg], 0.0).astype(BF16)
        bias = bs_ref[gg][:, :1]
        parts = []
        for c in range(tm // SGU_CHUNK):
            vc = vn[c * SGU_CHUNK:(c + 1) * SGU_CHUNK, :]
            parts.append(jnp.dot(w_s, vc, preferred_element_type=F32) + bias)
        sv = jnp.concatenate(parts, axis=0)
        p = (u_ref[:, gcols].astype(F32) * sv).astype(BF16)
        y = jnp.dot(p, wo_ref[gcols, :], preferred_element_type=F32)
        acc = y if acc is None else acc + y
    o_ref[...] += acc


def _sgu_out(x, z, ssq, v_gain, w_spatial, b_spatial, w_out, layer):
    t, d = x.shape
    tm = 512
    gps = SGU_GROUPS_PER_STEP
    gw = gps * SGU_GROUP_DIM
    n_steps = SGU_GROUPS // gps
    b_rep = jnp.broadcast_to(b_spatial[:, :, None], (SGU_GROUPS, SGU_CHUNK, LANES))
    kern = functools.partial(_sgu_out_kernel, tm=tm)
    return pl.pallas_call(
        kern,
        grid=(t // tm, n_steps),
        in_specs=[
            pl.BlockSpec((tm, d), lambda i, g: (i, 0)),
            pl.BlockSpec((tm, gw), lambda i, g: (i, g)),
            pl.BlockSpec((tm, gw), lambda i, g: (i, n_steps + g)),
            pl.BlockSpec((tm, LANES), lambda i, g: (i, 0)),
            pl.BlockSpec((1, gw), lambda i, g: (0, g)),
            pl.BlockSpec((gps, SGU_CHUNK, SGU_CHUNK), lambda i, g: (g, 0, 0)),
            pl.BlockSpec((gps, SGU_CHUNK, LANES), lambda i, g: (g, 0, 0)),
            pl.BlockSpec((None, gw, d), lambda i, g: (layer, g, 0)),
        ],
        out_specs=pl.BlockSpec((tm, d), lambda i, g: (i, 0)),
        out_shape=jax.ShapeDtypeStruct((t, d), F32),
        compiler_params=_params("parallel", "arbitrary"),
        name="sgu_out",
    )(x, z, z, ssq, v_gain.reshape(1, SGU_DIM), w_spatial, b_rep, w_out)


def kernel(x, ffn1_norm, ffn1_w_gate, ffn1_w_up, ffn1_w_down, mix_norm, ffn2_norm, ffn2_w_gate, ffn2_w_up, ffn2_w_down, attn_w_qkv, attn_q_gain, attn_k_gain, attn_w_out, pool_w_group, pool_scale, sgu_w_in, sgu_v_gain, sgu_w_spatial, sgu_b_spatial, sgu_w_out):
    batch, seq, d = x.shape
    assert d == D_MODEL and seq % MOBA_BLOCK == 0 and seq % SGU_CHUNK == 0
    xt = x.reshape(batch * seq, d)
    ffn1_f32 = (ffn1_w_gate, ffn1_w_up, ffn1_w_down)
    ffn2_f32 = (ffn2_w_gate, ffn2_w_up, ffn2_w_down)
    w_ffn = tuple(w[0].astype(BF16) for w in ffn1_f32)
    w_qkv, w_attn_out = attn_w_qkv.astype(BF16), attn_w_out.astype(BF16)
    w_sgu_in, w_sgu_out = sgu_w_in.astype(BF16), sgu_w_out.astype(BF16)
    for i in range(DEPTH):
        xt, w_ffn = _ffn(xt, ffn1_norm[i], w_ffn, ffn2_f32, i)
        kind, j = i % N_MIXERS, i // N_MIXERS
        if kind == 0:
            q, k, v = _qkv_proj(xt, mix_norm[i], w_qkv, j, attn_q_gain[j], attn_k_gain[j])
            o = _moba_attention(q, k, v, batch, seq)
            xt = _proj_residual(xt, o, w_attn_out, j)
        elif kind == 1:
            xt = _pool_mixer(xt, mix_norm[i], pool_w_group[j], pool_scale[j], batch, seq)
        else:
            z, ssq = _sgu_in(xt, mix_norm[i], w_sgu_in, j)
            xt = _sgu_out(xt, z, ssq, sgu_v_gain[j], sgu_w_spatial[j], sgu_b_spatial[j], w_sgu_out, j)
        if i + 1 < DEPTH:
            xt, w_ffn = _ffn(xt, ffn2_norm[i], w_ffn, ffn1_f32, i + 1)
        else:
            xt, _ = _ffn(xt, ffn2_norm[i], w_ffn)
    return xt.reshape(batch, seq, d)
```

```python
import functools

import jax
import jax.numpy as jnp
import numpy as np
from jax import lax
from jax.experimental import pallas as pl
from jax.experimental.pallas import tpu as pltpu

D_MODEL = 2048
DEPTH = 4
N_MIXERS = 3
RMS_EPS = 1e-6
D_FF = 5504
N_HEADS = 16
HEAD_DIM = D_MODEL // N_HEADS
MOBA_BLOCK = 256
MOBA_TOPK = 3
POOL_WINDOWS = (2, 4, 8, 16)
N_POOL_GROUPS = len(POOL_WINDOWS)
POOL_GROUP_DIM = D_MODEL // N_POOL_GROUPS
POOL_HALO = max(POOL_WINDOWS)
SGU_DIM = 3 * D_MODEL
SGU_CHUNK = 128
SGU_GROUPS = 8
SGU_GROUP_DIM = SGU_DIM // SGU_GROUPS

LANES = 128
SUBLANES = 8
MXU_DIM = 256
VMEM_LIMIT = 58 * 1024 * 1024

F32 = jnp.float32
BF16 = jnp.bfloat16
MASK_BIAS = -1e9
LOG2E = float(np.log2(np.e))
ATTN_GROUP = 4
ATTN_HEADS_PER_STEP = 2
ATTN_QBLOCKS_PER_STEP = 8
SGU_GROUPS_PER_STEP = 2
NT_DIMS = (((1,), (1,)), ((), ()))


SUB_N = 2 * MXU_DIM
FFN_TM = 1024
FFN_TF = 512


def _params(*sem):
    return pltpu.CompilerParams(dimension_semantics=sem, vmem_limit_bytes=VMEM_LIMIT)


def _rms_rows(x, gain):
    ms = jnp.mean(x * x, axis=-1, keepdims=True)
    return x * lax.rsqrt(ms + RMS_EPS) * gain


def _ffn_kernel(*refs, convert_next):
    if convert_next:
        x_ref, g_ref, wg_ref, wu_ref, wd_ref, *next_f32, o_ref, og_ref, ou_ref, od_ref, h_ref = refs
        for src, dst in zip(next_f32, (og_ref, ou_ref, od_ref)):
            dst[...] = src[...].astype(BF16)
    else:
        x_ref, g_ref, wg_ref, wu_ref, wd_ref, o_ref, h_ref = refs
    f = pl.program_id(1)
    tf = wd_ref.shape[0]

    def step(first):
        if first:
            x = x_ref[...]
            h = _rms_rows(x, g_ref[...]).astype(BF16)
            h_ref[...] = h
        else:
            h = h_ref[...]
        gate = jnp.dot(h, wg_ref[...], preferred_element_type=F32)
        up = jnp.dot(h, wu_ref[...], preferred_element_type=F32)
        a = (gate * jax.nn.sigmoid(gate)) * up * 0.5
        valid = D_FF - f * tf
        a = jnp.where(lax.broadcasted_iota(jnp.int32, (1, tf), 1) < valid, a, 0.0)
        wd = wd_ref[...]
        wd = jnp.where(lax.broadcasted_iota(jnp.int32, (tf, 1), 0) < valid, wd, jnp.zeros_like(wd))
        y = jnp.dot(a.astype(BF16), wd, preferred_element_type=F32)
        if first:
            o_ref[...] = x + y
        else:
            o_ref[...] += y

    pl.when(f == 0)(functools.partial(step, True))
    pl.when(f > 0)(functools.partial(step, False))


def _ffn(x, gain, weights, next_f32=None, next_layer=None):
    t, d = x.shape
    tm, tf = FFN_TM, FFN_TF
    n_tiles = t // tm
    in_specs = [
        pl.BlockSpec((tm, d), lambda i, f: (i, 0)),
        pl.BlockSpec((1, d), lambda i, f: (0, 0)),
        pl.BlockSpec((d, tf), lambda i, f: (0, f)),
        pl.BlockSpec((d, tf), lambda i, f: (0, f)),
        pl.BlockSpec((tf, d), lambda i, f: (f, 0)),
    ]
    out_specs = [pl.BlockSpec((tm, d), lambda i, f: (i, 0))]
    out_shape = [jax.ShapeDtypeStruct((t, d), F32)]
    operands = [x, gain.reshape(1, d), *weights]
    if next_f32 is not None:
        rows = d // n_tiles
        in_specs += [
            pl.BlockSpec((None, rows, tf), lambda i, f: (next_layer, i, f)),
            pl.BlockSpec((None, rows, tf), lambda i, f: (next_layer, i, f)),
            pl.BlockSpec((None, tf, rows), lambda i, f: (next_layer, f, i)),
        ]
        out_specs += [
            pl.BlockSpec((rows, tf), lambda i, f: (i, f)),
            pl.BlockSpec((rows, tf), lambda i, f: (i, f)),
            pl.BlockSpec((tf, rows), lambda i, f: (f, i)),
        ]
        out_shape += [jax.ShapeDtypeStruct(w.shape[1:], BF16) for w in next_f32]
        operands += list(next_f32)
    outs = pl.pallas_call(
        functools.partial(_ffn_kernel, convert_next=next_f32 is not None),
        grid=(n_tiles, pl.cdiv(D_FF, tf)),
        in_specs=in_specs,
        out_specs=out_specs,
        out_shape=out_shape,
        scratch_shapes=[pltpu.VMEM((tm, d), BF16)],
        compiler_params=_params("parallel", "arbitrary"),
        name="ffn",
    )(*operands)
    return outs[0], tuple(outs[1:])


def _qkv_kernel(x_ref, g_ref, w_ref, qg_ref, kg_ref, q_ref, k_ref, v_ref):
    d = q_ref.shape[1]
    h = _rms_rows(x_ref[...], g_ref[...]).astype(BF16)

    def head_norm(y, gain):
        cols = []
        for hh in range(y.shape[1] // HEAD_DIM):
            cols.append(_rms_rows(y[:, hh * HEAD_DIM:(hh + 1) * HEAD_DIM], gain))
        return jnp.concatenate(cols, axis=1)

    def project(section, out_ref, epilogue):
        for c0 in range(0, d, SUB_N):
            y = jnp.dot(h, w_ref[:, section * d + c0:section * d + c0 + SUB_N], preferred_element_type=F32)
            out_ref[:, c0:c0 + SUB_N] = epilogue(y).astype(out_ref.dtype)

    project(0, q_ref, lambda y: head_norm(y, qg_ref[...]))
    project(1, k_ref, lambda y: head_norm(y, kg_ref[...]))
    project(2, v_ref, lambda y: y)


def _qkv_proj(x, gain, w, layer, q_gain, k_gain):
    t, d = x.shape
    tm = 512
    return pl.pallas_call(
        _qkv_kernel,
        grid=(t // tm,),
        in_specs=[
            pl.BlockSpec((tm, d), lambda i: (i, 0)),
            pl.BlockSpec((1, d), lambda i: (0, 0)),
            pl.BlockSpec((None, d, 3 * d), lambda i: (layer, 0, 0), pipeline_mode=pl.Buffered(1)),
            pl.BlockSpec((1, HEAD_DIM), lambda i: (0, 0)),
            pl.BlockSpec((1, HEAD_DIM), lambda i: (0, 0)),
        ],
        out_specs=[
            pl.BlockSpec((tm, d), lambda i: (i, 0)),
            pl.BlockSpec((tm, d), lambda i: (i, 0)),
            pl.BlockSpec((tm, d), lambda i: (i, 0)),
        ],
        out_shape=[
            jax.ShapeDtypeStruct((t, d), F32),
            jax.ShapeDtypeStruct((t, d), BF16),
            jax.ShapeDtypeStruct((t, d), BF16),
        ],
        compiler_params=_params("parallel"),
        name="qkv_proj",
    )(x, gain.reshape(1, d), w, q_gain.reshape(1, HEAD_DIM), k_gain.reshape(1, HEAD_DIM))


def _attn_step(i, group, hh, out_rows, slope_ref, qa_ref, kt_ref, va_ref, o_ref):
    bs = MOBA_BLOCK
    c1 = HEAD_DIM ** -0.5 * LOG2E
    hcols = slice(hh * HEAD_DIM, (hh + 1) * HEAD_DIM)
    own = slice(i * bs, (i + 1) * bs)
    q_aug = qa_ref[hh, own, :]
    slope2 = slope_ref[hh] * LOG2E
    key_off = lax.broadcasted_iota(jnp.int32, (1, bs), 1)

    def col_bias(j):
        return slope2 * ((j - i) * bs + key_off).astype(F32)

    row = lax.broadcasted_iota(jnp.int32, (bs, bs), 0)
    col = lax.broadcasted_iota(jnp.int32, (bs, bs), 1)
    s = jnp.dot(q_aug[:, :HEAD_DIM], kt_ref[hh, :HEAD_DIM, own], preferred_element_type=F32)
    t_own = jnp.where(row >= col, s * c1 + col_bias(i), -jnp.inf)
    va_own = va_ref[hh, own, :]
    n_past = i

    def row_max(ts):
        tmax = functools.reduce(jnp.maximum, ts)
        return jnp.max(jnp.maximum(tmax[:, :LANES], tmax[:, LANES:]), axis=1, keepdims=True)

    m = acc = None
    n_groups = max(1, (n_past + group // 2) // group)
    bounds = [n_past * g // n_groups for g in range(n_groups + 1)]
    for j0, j1 in zip(bounds[:-1], bounds[1:]):
        ts = []
        if j1 > j0:
            s = jnp.dot(q_aug, kt_ref[hh, :, j0 * bs:j1 * bs], preferred_element_type=F32)
            ts = [s[:, (j - j0) * bs:(j - j0 + 1) * bs] * c1 + col_bias(j) for j in range(j0, j1)]
        if j0 == 0:
            m = row_max(ts + [t_own])
            acc = jnp.dot(jnp.exp2(t_own - m).astype(BF16), va_own, preferred_element_type=F32)
        else:
            m_new = jnp.maximum(m, row_max(ts))
            acc = jnp.exp2(m - m_new) * acc
            m = m_new
        if ts:
            pcat = jnp.concatenate([jnp.exp2(t - m).astype(BF16) for t in ts], axis=1)
            acc = acc + jnp.dot(pcat, va_ref[hh, j0 * bs:j1 * bs, :], preferred_element_type=F32)

    o_ref[out_rows, hcols] = (acc[:, :HEAD_DIM] / acc[:, HEAD_DIM:]).astype(o_ref.dtype)


def _attn_prepare_keys(hh, j, k_ref, v_ref, kt_ref, va_ref, kmean_ref):
    bs = MOBA_BLOCK
    hcols = slice(hh * HEAD_DIM, (hh + 1) * HEAD_DIM)
    rows = slice(j * bs, (j + 1) * bs)
    va_ref[hh, rows, :HEAD_DIM] = v_ref[rows, hcols]
    va_ref[hh, rows, HEAD_DIM:] = jnp.ones((bs, LANES), BF16)
    kj = k_ref[rows, hcols].astype(F32)
    kmean_ref[hh, j:j + 1, :] = jnp.mean(kj, axis=0, keepdims=True)
    kt_ref[hh, :HEAD_DIM, rows] = kj.T.astype(BF16)
    feat = lax.broadcasted_iota(jnp.int32, (LANES, bs), 0)
    kt_ref[hh, HEAD_DIM:, rows] = jnp.where(feat == j, 1.0, 0.0).astype(BF16)


def _attn_prepare_queries(hh, i, q_ref, qa_ref, kmean_ref):
    bs = MOBA_BLOCK
    hcols = slice(hh * HEAD_DIM, (hh + 1) * HEAD_DIM)
    rows = slice(i * bs, (i + 1) * bs)
    q32 = q_ref[rows, hcols]
    q_hi = q32.astype(BF16)
    qa_ref[hh, rows, :HEAD_DIM] = q_hi
    if i == 0:
        qa_ref[hh, rows, HEAD_DIM:] = jnp.zeros((bs, LANES), BF16)
        return
    n_rows = -(-i // SUBLANES) * SUBLANES
    q_lo = (q32 - q_hi.astype(F32)).astype(BF16)
    km = kmean_ref[hh, :n_rows, :]
    km_hi = km.astype(BF16)
    km_lo = (km - km_hi.astype(F32)).astype(BF16)

    def dot_nt(a, b):
        return lax.dot_general(a, b, NT_DIMS, preferred_element_type=F32)

    gate = dot_nt(km_hi, q_hi) + (dot_nt(km_hi, q_lo) + dot_nt(km_lo, q_hi))
    blk = lax.broadcasted_iota(jnp.int32, gate.shape, 0)
    rank = jnp.zeros(gate.shape, jnp.int32)
    for jp in range(i):
        gj = gate[jp:jp + 1, :]
        beats = jnp.logical_or(gj > gate, jnp.logical_and(gj == gate, blk > jp))
        rank = rank + jnp.where(beats, 1, 0)
    selected = jnp.logical_and(blk < i, rank < MOBA_TOPK)
    bias_t = jnp.where(selected, 0.0, MASK_BIAS).astype(F32)
    bias_t = jnp.concatenate([bias_t, jnp.full((LANES - n_rows, bs), MASK_BIAS, F32)], axis=0)
    qa_ref[hh, rows, HEAD_DIM:] = bias_t.T.astype(BF16)


def _attn_kernel(slope_ref, q_ref, k_ref, v_ref, o_ref, qa_ref, kt_ref, va_ref, kmean_ref, *, n_blocks,
                 group):
    bs = MOBA_BLOCK
    step = pl.program_id(2)
    heads = q_ref.shape[1] // HEAD_DIM
    q_blocks = o_ref.shape[0] // bs

    def prepare(blocks):
        for i in blocks:
            for hh in range(heads):
                _attn_prepare_keys(hh, i, k_ref, v_ref, kt_ref, va_ref, kmean_ref)
                _attn_prepare_queries(hh, i, q_ref, qa_ref, kmean_ref)

    for c in range(n_blocks // q_blocks):
        @pl.when(step == c)
        def _(c=c):
            first = c * q_blocks
            if c == 0:
                kmean_ref[...] = jnp.zeros_like(kmean_ref)
                prepare(range(q_blocks))
            for sub in range(q_blocks):
                for hh in range(heads):
                    _attn_step(first + sub, group, hh, slice(sub * bs, (sub + 1) * bs), slope_ref,
                               qa_ref, kt_ref, va_ref, o_ref)
            prepare(range(first + q_blocks, min(first + 2 * q_blocks, n_blocks)))


def _moba_attention(q, k, v, batch, seq):
    t, d = q.shape
    bs = MOBA_BLOCK
    n_blocks = seq // bs
    slopes = np.asarray(2.0 ** (-8.0 * np.arange(1, N_HEADS + 1) / N_HEADS), dtype=np.float32)
    slopes = jnp.asarray(np.broadcast_to(slopes[:, None, None], (N_HEADS, 1, bs)))
    kern = functools.partial(_attn_kernel, n_blocks=n_blocks, group=ATTN_GROUP)
    hps = ATTN_HEADS_PER_STEP
    hw = hps * HEAD_DIM
    qbs = ATTN_QBLOCKS_PER_STEP
    n_steps = n_blocks // qbs
    return pl.pallas_call(
        kern,
        grid=(batch, N_HEADS // hps, n_steps),
        in_specs=[
            pl.BlockSpec((hps, 1, bs), lambda b, h, i: (h, 0, 0)),
            pl.BlockSpec((seq, hw), lambda b, h, i: (b, h)),
            pl.BlockSpec((seq, hw), lambda b, h, i: (b, h)),
            pl.BlockSpec((seq, hw), lambda b, h, i: (b, h)),
        ],
        out_specs=pl.BlockSpec((qbs * bs, hw), lambda b, h, i: (b * n_steps + i, h)),
        out_shape=jax.ShapeDtypeStruct((t, d), BF16),
        scratch_shapes=[
            pltpu.VMEM((hps, seq, HEAD_DIM + LANES), BF16),
            pltpu.VMEM((hps, HEAD_DIM + LANES, seq), BF16),
            pltpu.VMEM((hps, seq, HEAD_DIM + LANES), BF16),
            pltpu.VMEM((hps, n_blocks, HEAD_DIM), F32),
        ],
        compiler_params=_params("parallel", "parallel", "arbitrary"),
        name="moba_attn",
    )(slopes, q, k, v)


def _proj_res_kernel(x_ref, a_ref, w_ref, o_ref):
    a = a_ref[...]
    for c0 in range(0, w_ref.shape[1], SUB_N):
        cols = slice(c0, c0 + SUB_N)
        o_ref[:, cols] = x_ref[:, cols] + jnp.dot(a, w_ref[:, cols], preferred_element_type=F32)


def _proj_residual(x, a, w, layer):
    t, n = x.shape
    kdim = a.shape[1]
    tm = 512
    return pl.pallas_call(
        _proj_res_kernel,
        grid=(t // tm,),
        in_specs=[
            pl.BlockSpec((tm, n), lambda i: (i, 0)),
            pl.BlockSpec((tm, kdim), lambda i: (i, 0)),
            pl.BlockSpec((None, kdim, n), lambda i: (layer, 0, 0)),
        ],
        out_specs=pl.BlockSpec((tm, n), lambda i: (i, 0)),
        out_shape=jax.ShapeDtypeStruct((t, n), F32),
        compiler_params=_params("parallel"),
        name="proj_residual",
    )(x, a, w)


def _pool_kernel(x_ref, halo_ref, g_ref, w_ref, sc_ref, o_ref, *, ts):
    i = pl.program_id(1)
    x = x_ref[...]
    gain = g_ref[...]
    h = _rms_rows(x, gain)
    h_halo = jnp.where(i == 0, 0.0, _rms_rows(halo_ref[...], gain))
    he = jnp.concatenate([h_halo, h], axis=0)
    pos = i * ts + lax.broadcasted_iota(jnp.int32, (ts, 1), 0)
    count = (pos + 1).astype(F32)
    cg = POOL_GROUP_DIM
    outs = []
    for g, win in enumerate(POOL_WINDOWS):
        a = he[:, g * cg:(g + 1) * cg]
        lead = 0
        width = 1
        while width < win:
            a = a[width:, :] + a[:-width, :]
            lead += width
            width *= 2
        start = POOL_HALO - lead
        pooled = a[start:start + ts, :] / jnp.minimum(count, float(win)) - h[:, g * cg:(g + 1) * cg]
        outs.append(jnp.dot(pooled.astype(BF16), w_ref[g], preferred_element_type=F32))
    y = jnp.concatenate(outs, axis=1)
    o_ref[...] = x + y * sc_ref[...]


def _pool_mixer(x, gain, w_group, scale, batch, seq):
    t, d = x.shape
    ts = 512
    per_seq = seq // ts
    halo_per_tile = ts // POOL_HALO
    kern = functools.partial(_pool_kernel, ts=ts)

    def halo_map(b, i):
        return (jnp.maximum((b * per_seq + i) * halo_per_tile - 1, 0), 0)

    return pl.pallas_call(
        kern,
        grid=(batch, per_seq),
        in_specs=[
            pl.BlockSpec((ts, d), lambda b, i: (b * per_seq + i, 0)),
            pl.BlockSpec((POOL_HALO, d), halo_map),
            pl.BlockSpec((1, d), lambda b, i: (0, 0)),
            pl.BlockSpec((N_POOL_GROUPS, POOL_GROUP_DIM, POOL_GROUP_DIM), lambda b, i: (0, 0, 0)),
            pl.BlockSpec((1, d), lambda b, i: (0, 0)),
        ],
        out_specs=pl.BlockSpec((ts, d), lambda b, i: (b * per_seq + i, 0)),
        out_shape=jax.ShapeDtypeStruct((t, d), F32),
        compiler_params=_params("parallel", "parallel"),
        name="pool_mixer",
    )(x, x, gain.reshape(1, d), w_group.astype(BF16), scale.reshape(1, d))


def _sgu_in_kernel(x_ref, g_ref, w_ref, z_ref, ssq_ref, h_ref, *, n_u):
    j = pl.program_id(1)

    def project(with_ssq, first=False):
        if first:
            h = _rms_rows(x_ref[...], g_ref[...]).astype(BF16)
            h_ref[...] = h
            ssq_ref[...] = jnp.zeros_like(ssq_ref)
        else:
            h = h_ref[...]
        part = None
        for c0 in range(0, w_ref.shape[1], SUB_N):
            y = jnp.dot(h, w_ref[:, c0:c0 + SUB_N], preferred_element_type=F32)
            z = 0.5 * y * (1.0 + lax.erf(y * np.float32(np.sqrt(0.5))))
            z_ref[:, c0:c0 + SUB_N] = z.astype(BF16)
            if with_ssq:
                zz = z * z
                for c in range(SUB_N // LANES):
                    sl = zz[:, c * LANES:(c + 1) * LANES]
                    part = sl if part is None else part + sl
        if with_ssq:
            ssq_ref[...] += part

    pl.when(j == 0)(functools.partial(project, False, first=True))
    pl.when(jnp.logical_and(j > 0, j < n_u))(functools.partial(project, False))
    pl.when(j >= n_u)(functools.partial(project, True))


def _sgu_in(x, gain, w_in, layer):
    t, d = x.shape
    n = w_in.shape[2]
    tm, tn = 1024, 2048
    n_u = (n // 2) // tn
    kern = functools.partial(_sgu_in_kernel, n_u=n_u)
    return pl.pallas_call(
        kern,
        grid=(t // tm, n // tn),
        in_specs=[
            pl.BlockSpec((tm, d), lambda i, j: (i, 0)),
            pl.BlockSpec((1, d), lambda i, j: (0, 0)),
            pl.BlockSpec((None, d, tn), lambda i, j: (layer, 0, j)),
        ],
        out_specs=[
            pl.BlockSpec((tm, tn), lambda i, j: (i, j)),
            pl.BlockSpec((tm, LANES), lambda i, j: (i, 0)),
        ],
        out_shape=[
            jax.ShapeDtypeStruct((t, n), BF16),
            jax.ShapeDtypeStruct((t, LANES), F32),
        ],
        scratch_shapes=[pltpu.VMEM((tm, d), BF16)],
        compiler_params=_params("parallel", "arbitrary"),
        name="sgu_in",
    )(x, gain.reshape(1, d), w_in)


def _sgu_out_kernel(x_ref, u_ref, v_ref, ssq_ref, vg_ref, ws_ref, bs_ref, wo_ref, o_ref, *, tm):
    g = pl.program_id(1)

    @pl.when(g == 0)
    def _():
        o_ref[...] = x_ref[...]

    inv = lax.rsqrt(jnp.sum(ssq_ref[...], axis=1, keepdims=True) / SGU_DIM + RMS_EPS)
    row = lax.broadcasted_iota(jnp.int32, (SGU_CHUNK, SGU_CHUNK), 0)
    col = lax.broadcasted_iota(jnp.int32, (SGU_CHUNK, SGU_CHUNK), 1)
    gd = SGU_GROUP_DIM
    acc = None
    for gg in range(ws_ref.shape[0]):
        gcols = slice(gg * gd, (gg + 1) * gd)
        vn = (v_ref[:, gcols].astype(F32) * inv * vg_ref[:, gcols]).astype(BF16)
        w_s = jnp.where(row >= col, ws_ref[gg], 0.0).astype(BF16)
        bias = bs_ref[gg][:, :1]
        parts = []
        for c in range(tm // SGU_CHUNK):
            vc = vn[c * SGU_CHUNK:(c + 1) * SGU_CHUNK, :]
            parts.append(jnp.dot(w_s, vc, preferred_element_type=F32) + bias)
        sv = jnp.concatenate(parts, axis=0)
        p = (u_ref[:, gcols].astype(F32) * sv).astype(BF16)
        y = jnp.dot(p, wo_ref[gcols, :], preferred_element_type=F32)
        acc = y if acc is None else acc + y
    o_ref[...] += acc


def _sgu_out(x, z, ssq, v_gain, w_spatial, b_spatial, w_out, layer):
    t, d = x.shape
    tm = 512
    gps = SGU_GROUPS_PER_STEP
    gw = gps * SGU_GROUP_DIM
    n_steps = SGU_GROUPS // gps
    b_rep = jnp.broadcast_to(b_spatial[:, :, None], (SGU_GROUPS, SGU_CHUNK, LANES))
    kern = functools.partial(_sgu_out_kernel, tm=tm)
    return pl.pallas_call(
        kern,
        grid=(t // tm, n_steps),
        in_specs=[
            pl.BlockSpec((tm, d), lambda i, g: (i, 0)),
            pl.BlockSpec((tm, gw), lambda i, g: (i, g)),
            pl.BlockSpec((tm, gw), lambda i, g: (i, n_steps + g)),
            pl.BlockSpec((tm, LANES), lambda i, g: (i, 0)),
            pl.BlockSpec((1, gw), lambda i, g: (0, g)),
            pl.BlockSpec((gps, SGU_CHUNK, SGU_CHUNK), lambda i, g: (g, 0, 0)),
            pl.BlockSpec((gps, SGU_CHUNK, LANES), lambda i, g: (g, 0, 0)),
            pl.BlockSpec((None, gw, d), lambda i, g: (layer, g, 0)),
        ],
        out_specs=pl.BlockSpec((tm, d), lambda i, g: (i, 0)),
        out_shape=jax.ShapeDtypeStruct((t, d), F32),
        compiler_params=_params("parallel", "arbitrary"),
        name="sgu_out",
    )(x, z, z, ssq, v_gain.reshape(1, SGU_DIM), w_spatial, b_rep, w_out)


def kernel(x, ffn1_norm, ffn1_w_gate, ffn1_w_up, ffn1_w_down, mix_norm, ffn2_norm, ffn2_w_gate, ffn2_w_up, ffn2_w_down, attn_w_qkv, attn_q_gain, attn_k_gain, attn_w_out, pool_w_group, pool_scale, sgu_w_in, sgu_v_gain, sgu_w_spatial, sgu_b_spatial, sgu_w_out):
    batch, seq, d = x.shape
    assert d == D_MODEL and seq % MOBA_BLOCK == 0 and seq % SGU_CHUNK == 0
    xt = x.reshape(batch * seq, d)
    ffn1_f32 = (ffn1_w_gate, ffn1_w_up, ffn1_w_down)
    ffn2_f32 = (ffn2_w_gate, ffn2_w_up, ffn2_w_down)
    w_ffn = tuple(w[0].astype(BF16) for w in ffn1_f32)
    w_qkv, w_attn_out = attn_w_qkv.astype(BF16), attn_w_out.astype(BF16)
    w_sgu_in, w_sgu_out = sgu_w_in.astype(BF16), sgu_w_out.astype(BF16)
    for i in range(DEPTH):
        xt, w_ffn = _ffn(xt, ffn1_norm[i], w_ffn, ffn2_f32, i)
        kind, j = i % N_MIXERS, i // N_MIXERS
        if kind == 0:
            q, k, v = _qkv_proj(xt, mix_norm[i], w_qkv, j, attn_q_gain[j], attn_k_gain[j])
            o = _moba_attention(q, k, v, batch, seq)
            xt = _proj_residual(xt, o, w_attn_out, j)
        elif kind == 1:
            xt = _pool_mixer(xt, mix_norm[i], pool_w_group[j], pool_scale[j], batch, seq)
        else:
            z, ssq = _sgu_in(xt, mix_norm[i], w_sgu_in, j)
            xt = _sgu_out(xt, z, ssq, sgu_v_gain[j], sgu_w_spatial[j], sgu_b_spatial[j], w_sgu_out, j)
        if i + 1 < DEPTH:
            xt, w_ffn = _ffn(xt, ffn2_norm[i], w_ffn, ffn1_f32, i + 1)
        else:
            xt, _ = _ffn(xt, ffn2_norm[i], w_ffn)
    return xt.reshape(batch, seq, d)
```

```python
import functools

import jax
import jax.numpy as jnp
import numpy as np
from jax import lax
from jax.experimental import pallas as pl
from jax.experimental.pallas import tpu as pltpu

D_MODEL = 2048
DEPTH = 4
N_MIXERS = 3
RMS_EPS = 1e-6
D_FF = 5504
N_HEADS = 16
HEAD_DIM = D_MODEL // N_HEADS
MOBA_BLOCK = 256
MOBA_TOPK = 3
POOL_WINDOWS = (2, 4, 8, 16)
N_POOL_GROUPS = len(POOL_WINDOWS)
POOL_GROUP_DIM = D_MODEL // N_POOL_GROUPS
POOL_HALO = max(POOL_WINDOWS)
SGU_DIM = 3 * D_MODEL
SGU_CHUNK = 128
SGU_GROUPS = 8
SGU_GROUP_DIM = SGU_DIM // SGU_GROUPS

LANES = 128
SUBLANES = 8
MXU_DIM = 256
VMEM_LIMIT = 58 * 1024 * 1024

F32 = jnp.float32
BF16 = jnp.bfloat16
MASK_BIAS = -1e9
LOG2E = float(np.log2(np.e))
ATTN_GROUP = 4
ATTN_HEADS_PER_STEP = 2
ATTN_QBLOCKS_PER_STEP = 8
SGU_GROUPS_PER_STEP = 2
NT_DIMS = (((1,), (1,)), ((), ()))


SUB_N = 2 * MXU_DIM
FFN_TM = 1024
FFN_TF = 512


def _params(*sem):
    return pltpu.CompilerParams(dimension_semantics=sem, vmem_limit_bytes=VMEM_LIMIT)


def _rms_rows(x, gain):
    ms = jnp.mean(x * x, axis=-1, keepdims=True)
    return x * lax.rsqrt(ms + RMS_EPS) * gain


def _ffn_kernel(*refs, convert_next):
    if convert_next:
        x_ref, g_ref, wg_ref, wu_ref, wd_ref, *next_f32, o_ref, og_ref, ou_ref, od_ref, h_ref = refs
        for src, dst in zip(next_f32, (og_ref, ou_ref, od_ref)):
            dst[...] = src[...].astype(BF16)
    else:
        x_ref, g_ref, wg_ref, wu_ref, wd_ref, o_ref, h_ref = refs
    f = pl.program_id(1)
    tf = wd_ref.shape[0]

    def step(first):
        if first:
            x = x_ref[...]
            h = _rms_rows(x, g_ref[...]).astype(BF16)
            h_ref[...] = h
        else:
            h = h_ref[...]
        gate = jnp.dot(h, wg_ref[...], preferred_element_type=F32)
        up = jnp.dot(h, wu_ref[...], preferred_element_type=F32)
        a = (gate * jax.nn.sigmoid(gate)) * up * 0.5
        valid = D_FF - f * tf
        a = jnp.where(lax.broadcasted_iota(jnp.int32, (1, tf), 1) < valid, a, 0.0)
        wd = wd_ref[...]
        wd = jnp.where(lax.broadcasted_iota(jnp.int32, (tf, 1), 0) < valid, wd, jnp.zeros_like(wd))
        y = jnp.dot(a.astype(BF16), wd, preferred_element_type=F32)
        if first:
            o_ref[...] = x + y
        else:
            o_ref[...] += y

    pl.when(f == 0)(functools.partial(step, True))
    pl.when(f > 0)(functools.partial(step, False))


def _ffn(x, gain, weights, next_f32=None, next_layer=None):
    t, d = x.shape
    tm, tf = FFN_TM, FFN_TF
    n_tiles = t // tm
    in_specs = [
        pl.BlockSpec((tm, d), lambda i, f: (i, 0)),
        pl.BlockSpec((1, d), lambda i, f: (0, 0)),
        pl.BlockSpec((None, d, tf), lambda i, f: (f, 0, 0)),
        pl.BlockSpec((None, d, tf), lambda i, f: (f, 0, 0)),
        pl.BlockSpec((tf, d), lambda i, f: (f, 0)),
    ]
    out_specs = [pl.BlockSpec((tm, d), lambda i, f: (i, 0))]
    out_shape = [jax.ShapeDtypeStruct((t, d), F32)]
    operands = [x, gain.reshape(1, d), *weights]
    if next_f32 is not None:
        rows = d // n_tiles
        in_specs += [
            pl.BlockSpec((None, rows, tf), lambda i, f: (next_layer, i, f)),
            pl.BlockSpec((None, rows, tf), lambda i, f: (next_layer, i, f)),
            pl.BlockSpec((None, tf, rows), lambda i, f: (next_layer, f, i)),
        ]
        out_specs += [
            pl.BlockSpec((None, rows, tf), lambda i, f: (f, i, 0)),
            pl.BlockSpec((None, rows, tf), lambda i, f: (f, i, 0)),
            pl.BlockSpec((tf, rows), lambda i, f: (f, i)),
        ]
        blocked = jax.ShapeDtypeStruct((pl.cdiv(D_FF, tf), d, tf), BF16)
        out_shape += [blocked, blocked, jax.ShapeDtypeStruct(next_f32[2].shape[1:], BF16)]
        operands += list(next_f32)
    outs = pl.pallas_call(
        functools.partial(_ffn_kernel, convert_next=next_f32 is not None),
        grid=(n_tiles, pl.cdiv(D_FF, tf)),
        in_specs=in_specs,
        out_specs=out_specs,
        out_shape=out_shape,
        scratch_shapes=[pltpu.VMEM((tm, d), BF16)],
        compiler_params=_params("parallel", "arbitrary"),
        name="ffn",
    )(*operands)
    return outs[0], tuple(outs[1:])


def _qkv_kernel(x_ref, g_ref, w_ref, qg_ref, kg_ref, q_ref, k_ref, v_ref):
    d = q_ref.shape[1]
    h = _rms_rows(x_ref[...], g_ref[...]).astype(BF16)

    def head_norm(y, gain):
        cols = []
        for hh in range(y.shape[1] // HEAD_DIM):
            cols.append(_rms_rows(y[:, hh * HEAD_DIM:(hh + 1) * HEAD_DIM], gain))
        return jnp.concatenate(cols, axis=1)

    def project(section, out_ref, epilogue):
        for c0 in range(0, d, SUB_N):
            y = jnp.dot(h, w_ref[:, section * d + c0:section * d + c0 + SUB_N], preferred_element_type=F32)
            out_ref[:, c0:c0 + SUB_N] = epilogue(y).astype(out_ref.dtype)

    project(0, q_ref, lambda y: head_norm(y, qg_ref[...]))
    project(1, k_ref, lambda y: head_norm(y, kg_ref[...]))
    project(2, v_ref, lambda y: y)


def _qkv_proj(x, gain, w, layer, q_gain, k_gain):
    t, d = x.shape
    tm = 512
    return pl.pallas_call(
        _qkv_kernel,
        grid=(t // tm,),
        in_specs=[
            pl.BlockSpec((tm, d), lambda i: (i, 0)),
            pl.BlockSpec((1, d), lambda i: (0, 0)),
            pl.BlockSpec((None, d, 3 * d), lambda i: (layer, 0, 0), pipeline_mode=pl.Buffered(1)),
            pl.BlockSpec((1, HEAD_DIM), lambda i: (0, 0)),
            pl.BlockSpec((1, HEAD_DIM), lambda i: (0, 0)),
        ],
        out_specs=[
            pl.BlockSpec((tm, d), lambda i: (i, 0)),
            pl.BlockSpec((tm, d), lambda i: (i, 0)),
            pl.BlockSpec((tm, d), lambda i: (i, 0)),
        ],
        out_shape=[
            jax.ShapeDtypeStruct((t, d), F32),
            jax.ShapeDtypeStruct((t, d), BF16),
            jax.ShapeDtypeStruct((t, d), BF16),
        ],
        compiler_params=_params("parallel"),
        name="qkv_proj",
    )(x, gain.reshape(1, d), w, q_gain.reshape(1, HEAD_DIM), k_gain.reshape(1, HEAD_DIM))


def _attn_step(i, group, hh, out_rows, slope_ref, qa_ref, kt_ref, va_ref, o_ref):
    bs = MOBA_BLOCK
    c1 = HEAD_DIM ** -0.5 * LOG2E
    hcols = slice(hh * HEAD_DIM, (hh + 1) * HEAD_DIM)
    own = slice(i * bs, (i + 1) * bs)
    q_aug = qa_ref[hh, own, :]
    slope2 = slope_ref[hh] * LOG2E
    key_off = lax.broadcasted_iota(jnp.int32, (1, bs), 1)

    def col_bias(j):
        return slope2 * ((j - i) * bs + key_off).astype(F32)

    row = lax.broadcasted_iota(jnp.int32, (bs, bs), 0)
    col = lax.broadcasted_iota(jnp.int32, (bs, bs), 1)
    s = jnp.dot(q_aug[:, :HEAD_DIM], kt_ref[hh, :HEAD_DIM, own], preferred_element_type=F32)
    t_own = jnp.where(row >= col, s * c1 + col_bias(i), -jnp.inf)
    va_own = va_ref[hh, own, :]
    n_past = i

    def row_max(ts):
        tmax = functools.reduce(jnp.maximum, ts)
        return jnp.max(jnp.maximum(tmax[:, :LANES], tmax[:, LANES:]), axis=1, keepdims=True)

    m = acc = None
    n_groups = max(1, (n_past + group // 2) // group)
    bounds = [n_past * g // n_groups for g in range(n_groups + 1)]
    for j0, j1 in zip(bounds[:-1], bounds[1:]):
        ts = []
        if j1 > j0:
            s = jnp.dot(q_aug, kt_ref[hh, :, j0 * bs:j1 * bs], preferred_element_type=F32)
            ts = [s[:, (j - j0) * bs:(j - j0 + 1) * bs] * c1 + col_bias(j) for j in range(j0, j1)]
        if j0 == 0:
            m = row_max(ts + [t_own])
            acc = jnp.dot(jnp.exp2(t_own - m).astype(BF16), va_own, preferred_element_type=F32)
        else:
            m_new = jnp.maximum(m, row_max(ts))
            acc = jnp.exp2(m - m_new) * acc
            m = m_new
        if ts:
            pcat = jnp.concatenate([jnp.exp2(t - m).astype(BF16) for t in ts], axis=1)
            acc = acc + jnp.dot(pcat, va_ref[hh, j0 * bs:j1 * bs, :], preferred_element_type=F32)

    o_ref[out_rows, hcols] = (acc[:, :HEAD_DIM] / acc[:, HEAD_DIM:]).astype(o_ref.dtype)


def _attn_prepare_keys(hh, j, k_ref, v_ref, kt_ref, va_ref, kmean_ref):
    bs = MOBA_BLOCK
    hcols = slice(hh * HEAD_DIM, (hh + 1) * HEAD_DIM)
    rows = slice(j * bs, (j + 1) * bs)
    va_ref[hh, rows, :HEAD_DIM] = v_ref[rows, hcols]
    va_ref[hh, rows, HEAD_DIM:] = jnp.ones((bs, LANES), BF16)
    kj = k_ref[rows, hcols].astype(F32)
    kmean_ref[hh, j:j + 1, :] = jnp.mean(kj, axis=0, keepdims=True)
    kt_ref[hh, :HEAD_DIM, rows] = kj.T.astype(BF16)
    feat = lax.broadcasted_iota(jnp.int32, (LANES, bs), 0)
    kt_ref[hh, HEAD_DIM:, rows] = jnp.where(feat == j, 1.0, 0.0).astype(BF16)


def _attn_prepare_queries(hh, i, q_ref, qa_ref, kmean_ref):
    bs = MOBA_BLOCK
    hcols = slice(hh * HEAD_DIM, (hh + 1) * HEAD_DIM)
    rows = slice(i * bs, (i + 1) * bs)
    q32 = q_ref[rows, hcols]
    q_hi = q32.astype(BF16)
    qa_ref[hh, rows, :HEAD_DIM] = q_hi
    if i == 0:
        qa_ref[hh, rows, HEAD_DIM:] = jnp.zeros((bs, LANES), BF16)
        return
    n_rows = -(-i // SUBLANES) * SUBLANES
    q_lo = (q32 - q_hi.astype(F32)).astype(BF16)
    km = kmean_ref[hh, :n_rows, :]
    km_hi = km.astype(BF16)
    km_lo = (km - km_hi.astype(F32)).astype(BF16)

    def dot_nt(a, b):
        return lax.dot_general(a, b, NT_DIMS, preferred_element_type=F32)

    gate = dot_nt(km_hi, q_hi) + (dot_nt(km_hi, q_lo) + dot_nt(km_lo, q_hi))
    blk = lax.broadcasted_iota(jnp.int32, gate.shape, 0)
    rank = jnp.zeros(gate.shape, jnp.int32)
    for jp in range(i):
        gj = gate[jp:jp + 1, :]
        beats = jnp.logical_or(gj > gate, jnp.logical_and(gj == gate, blk > jp))
        rank = rank + jnp.where(beats, 1, 0)
    selected = jnp.logical_and(blk < i, rank < MOBA_TOPK)
    bias_t = jnp.where(selected, 0.0, MASK_BIAS).astype(F32)
    bias_t = jnp.concatenate([bias_t, jnp.full((LANES - n_rows, bs), MASK_BIAS, F32)], axis=0)
    qa_ref[hh, rows, HEAD_DIM:] = bias_t.T.astype(BF16)


def _attn_kernel(slope_ref, q_ref, k_ref, v_ref, o_ref, qa_ref, kt_ref, va_ref, kmean_ref, *, n_blocks,
                 group):
    bs = MOBA_BLOCK
    step = pl.program_id(2)
    heads = q_ref.shape[1] // HEAD_DIM
    q_blocks = o_ref.shape[0] // bs

    def prepare(blocks):
        for i in blocks:
            for hh in range(heads):
                _attn_prepare_keys(hh, i, k_ref, v_ref, kt_ref, va_ref, kmean_ref)
                _attn_prepare_queries(hh, i, q_ref, qa_ref, kmean_ref)

    for c in range(n_blocks // q_blocks):
        @pl.when(step == c)
        def _(c=c):
            first = c * q_blocks
            if c == 0:
                kmean_ref[...] = jnp.zeros_like(kmean_ref)
                prepare(range(q_blocks))
            for sub in range(q_blocks):
                for hh in range(heads):
                    _attn_step(first + sub, group, hh, slice(sub * bs, (sub + 1) * bs), slope_ref,
                               qa_ref, kt_ref, va_ref, o_ref)
            prepare(range(first + q_blocks, min(first + 2 * q_blocks, n_blocks)))


def _moba_attention(q, k, v, batch, seq):
    t, d = q.shape
    bs = MOBA_BLOCK
    n_blocks = seq // bs
    slopes = np.asarray(2.0 ** (-8.0 * np.arange(1, N_HEADS + 1) / N_HEADS), dtype=np.float32)
    slopes = jnp.asarray(np.broadcast_to(slopes[:, None, None], (N_HEADS, 1, bs)))
    kern = functools.partial(_attn_kernel, n_blocks=n_blocks, group=ATTN_GROUP)
    hps = ATTN_HEADS_PER_STEP
    hw = hps * HEAD_DIM
    qbs = ATTN_QBLOCKS_PER_STEP
    n_steps = n_blocks // qbs
    return pl.pallas_call(
        kern,
        grid=(batch, N_HEADS // hps, n_steps),
        in_specs=[
            pl.BlockSpec((hps, 1, bs), lambda b, h, i: (h, 0, 0)),
            pl.BlockSpec((seq, hw), lambda b, h, i: (b, h)),
            pl.BlockSpec((seq, hw), lambda b, h, i: (b, h)),
            pl.BlockSpec((seq, hw), lambda b, h, i: (b, h)),
        ],
        out_specs=pl.BlockSpec((qbs * bs, hw), lambda b, h, i: (b * n_steps + i, h)),
        out_shape=jax.ShapeDtypeStruct((t, d), BF16),
        scratch_shapes=[
            pltpu.VMEM((hps, seq, HEAD_DIM + LANES), BF16),
            pltpu.VMEM((hps, HEAD_DIM + LANES, seq), BF16),
            pltpu.VMEM((hps, seq, HEAD_DIM + LANES), BF16),
            pltpu.VMEM((hps, n_blocks, HEAD_DIM), F32),
        ],
        compiler_params=_params("parallel", "parallel", "arbitrary"),
        name="moba_attn",
    )(slopes, q, k, v)


def _proj_res_kernel(x_ref, a_ref, w_ref, o_ref):
    a = a_ref[...]
    for c0 in range(0, w_ref.shape[1], SUB_N):
        cols = slice(c0, c0 + SUB_N)
        o_ref[:, cols] = x_ref[:, cols] + jnp.dot(a, w_ref[:, cols], preferred_element_type=F32)


def _proj_residual(x, a, w, layer):
    t, n = x.shape
    kdim = a.shape[1]
    tm = 512
    return pl.pallas_call(
        _proj_res_kernel,
        grid=(t // tm,),
        in_specs=[
            pl.BlockSpec((tm, n), lambda i: (i, 0)),
            pl.BlockSpec((tm, kdim), lambda i: (i, 0)),
            pl.BlockSpec((None, kdim, n), lambda i: (layer, 0, 0)),
        ],
        out_specs=pl.BlockSpec((tm, n), lambda i: (i, 0)),
        out_shape=jax.ShapeDtypeStruct((t, n), F32),
        compiler_params=_params("parallel"),
        name="proj_residual",
    )(x, a, w)


def _pool_kernel(x_ref, halo_ref, g_ref, w_ref, sc_ref, o_ref, *, ts):
    i = pl.program_id(1)
    x = x_ref[...]
    gain = g_ref[...]
    h = _rms_rows(x, gain)
    h_halo = jnp.where(i == 0, 0.0, _rms_rows(halo_ref[...], gain))
    he = jnp.concatenate([h_halo, h], axis=0)
    pos = i * ts + lax.broadcasted_iota(jnp.int32, (ts, 1), 0)
    count = (pos + 1).astype(F32)
    cg = POOL_GROUP_DIM
    outs = []
    for g, win in enumerate(POOL_WINDOWS):
        a = he[:, g * cg:(g + 1) * cg]
        lead = 0
        width = 1
        while width < win:
            a = a[width:, :] + a[:-width, :]
            lead += width
            width *= 2
        start = POOL_HALO - lead
        pooled = a[start:start + ts, :] / jnp.minimum(count, float(win)) - h[:, g * cg:(g + 1) * cg]
        outs.append(jnp.dot(pooled.astype(BF16), w_ref[g], preferred_element_type=F32))
    y = jnp.concatenate(outs, axis=1)
    o_ref[...] = x + y * sc_ref[...]


def _pool_mixer(x, gain, w_group, scale, batch, seq):
    t, d = x.shape
    ts = 512
    per_seq = seq // ts
    halo_per_tile = ts // POOL_HALO
    kern = functools.partial(_pool_kernel, ts=ts)

    def halo_map(b, i):
        return (jnp.maximum((b * per_seq + i) * halo_per_tile - 1, 0), 0)

    return pl.pallas_call(
        kern,
        grid=(batch, per_seq),
        in_specs=[
            pl.BlockSpec((ts, d), lambda b, i: (b * per_seq + i, 0)),
            pl.BlockSpec((POOL_HALO, d), halo_map),
            pl.BlockSpec((1, d), lambda b, i: (0, 0)),
            pl.BlockSpec((N_POOL_GROUPS, POOL_GROUP_DIM, POOL_GROUP_DIM), lambda b, i: (0, 0, 0)),
            pl.BlockSpec((1, d), lambda b, i: (0, 0)),
        ],
        out_specs=pl.BlockSpec((ts, d), lambda b, i: (b * per_seq + i, 0)),
        out_shape=jax.ShapeDtypeStruct((t, d), F32),
        compiler_params=_params("parallel", "parallel"),
        name="pool_mixer",
    )(x, x, gain.reshape(1, d), w_group.astype(BF16), scale.reshape(1, d))


def _sgu_in_kernel(x_ref, g_ref, w_ref, z_ref, ssq_ref, h_ref, *, n_u):
    j = pl.program_id(1)

    def project(with_ssq, first=False):
        if first:
            h = _rms_rows(x_ref[...], g_ref[...]).astype(BF16)
            h_ref[...] = h
            ssq_ref[...] = jnp.zeros_like(ssq_ref)
        else:
            h = h_ref[...]
        part = None
        for c0 in range(0, w_ref.shape[1], SUB_N):
            y = jnp.dot(h, w_ref[:, c0:c0 + SUB_N], preferred_element_type=F32)
            z = 0.5 * y * (1.0 + lax.erf(y * np.float32(np.sqrt(0.5))))
            z_ref[:, c0:c0 + SUB_N] = z.astype(BF16)
            if with_ssq:
                zz = z * z
                for c in range(SUB_N // LANES):
                    sl = zz[:, c * LANES:(c + 1) * LANES]
                    part = sl if part is None else part + sl
        if with_ssq:
            ssq_ref[...] += part

    pl.when(j == 0)(functools.partial(project, False, first=True))
    pl.when(jnp.logical_and(j > 0, j < n_u))(functools.partial(project, False))
    pl.when(j >= n_u)(functools.partial(project, True))


def _sgu_in(x, gain, w_in, layer):
    t, d = x.shape
    n = w_in.shape[2]
    tm, tn = 1024, 2048
    n_u = (n // 2) // tn
    kern = functools.partial(_sgu_in_kernel, n_u=n_u)
    return pl.pallas_call(
        kern,
        grid=(t // tm, n // tn),
        in_specs=[
            pl.BlockSpec((tm, d), lambda i, j: (i, 0)),
            pl.BlockSpec((1, d), lambda i, j: (0, 0)),
            pl.BlockSpec((None, d, tn), lambda i, j: (layer, 0, j)),
        ],
        out_specs=[
            pl.BlockSpec((tm, tn), lambda i, j: (i, j)),
            pl.BlockSpec((tm, LANES), lambda i, j: (i, 0)),
        ],
        out_shape=[
            jax.ShapeDtypeStruct((t, n), BF16),
            jax.ShapeDtypeStruct((t, LANES), F32),
        ],
        scratch_shapes=[pltpu.VMEM((tm, d), BF16)],
        compiler_params=_params("parallel", "arbitrary"),
        name="sgu_in",
    )(x, gain.reshape(1, d), w_in)


def _sgu_out_kernel(x_ref, u_ref, v_ref, ssq_ref, vg_ref, ws_ref, bs_ref, wo_ref, o_ref, *, tm):
    g = pl.program_id(1)

    @pl.when(g == 0)
    def _():
        o_ref[...] = x_ref[...]

    inv = lax.rsqrt(jnp.sum(ssq_ref[...], axis=1, keepdims=True) / SGU_DIM + RMS_EPS)
    row = lax.broadcasted_iota(jnp.int32, (SGU_CHUNK, SGU_CHUNK), 0)
    col = lax.broadcasted_iota(jnp.int32, (SGU_CHUNK, SGU_CHUNK), 1)
    gd = SGU_GROUP_DIM
    acc = None
    for gg in range(ws_ref.shape[0]):
        gcols = slice(gg * gd, (gg + 1) * gd)
        vn = (v_ref[:, gcols].astype(F32) * inv * vg_ref[:, gcols]).astype(BF16)
        w_s = jnp.where(row >= col, ws_ref[gg], 0.0).astype(BF16)
        bias = bs_ref[gg][:, :1]
        parts = []
        for c in range(tm // SGU_CHUNK):
            vc = vn[c * SGU_CHUNK:(c + 1) * SGU_CHUNK, :]
            parts.append(jnp.dot(w_s, vc, preferred_element_type=F32) + bias)
        sv = jnp.concatenate(parts, axis=0)
        p = (u_ref[:, gcols].astype(F32) * sv).astype(BF16)
        y = jnp.dot(p, wo_ref[gcols, :], preferred_element_type=F32)
        acc = y if acc is None else acc + y
    o_ref[...] += acc


def _sgu_out(x, z, ssq, v_gain, w_spatial, b_spatial, w_out, layer):
    t, d = x.shape
    tm = 512
    gps = SGU_GROUPS_PER_STEP
    gw = gps * SGU_GROUP_DIM
    n_steps = SGU_GROUPS // gps
    b_rep = jnp.broadcast_to(b_spatial[:, :, None], (SGU_GROUPS, SGU_CHUNK, LANES))
    kern = functools.partial(_sgu_out_kernel, tm=tm)
    return pl.pallas_call(
        kern,
        grid=(t // tm, n_steps),
        in_specs=[
            pl.BlockSpec((tm, d), lambda i, g: (i, 0)),
            pl.BlockSpec((tm, gw), lambda i, g: (i, g)),
            pl.BlockSpec((tm, gw), lambda i, g: (i, n_steps + g)),
            pl.BlockSpec((tm, LANES), lambda i, g: (i, 0)),
            pl.BlockSpec((1, gw), lambda i, g: (0, g)),
            pl.BlockSpec((gps, SGU_CHUNK, SGU_CHUNK), lambda i, g: (g, 0, 0)),
            pl.BlockSpec((gps, SGU_CHUNK, LANES), lambda i, g: (g, 0, 0)),
            pl.BlockSpec((None, gw, d), lambda i, g: (layer, g, 0)),
        ],
        out_specs=pl.BlockSpec((tm, d), lambda i, g: (i, 0)),
        out_shape=jax.ShapeDtypeStruct((t, d), F32),
        compiler_params=_params("parallel", "arbitrary"),
        name="sgu_out",
    )(x, z, z, ssq, v_gain.reshape(1, SGU_DIM), w_spatial, b_rep, w_out)


def kernel(x, ffn1_norm, ffn1_w_gate, ffn1_w_up, ffn1_w_down, mix_norm, ffn2_norm, ffn2_w_gate, ffn2_w_up, ffn2_w_down, attn_w_qkv, attn_q_gain, attn_k_gain, attn_w_out, pool_w_group, pool_scale, sgu_w_in, sgu_v_gain, sgu_w_spatial, sgu_b_spatial, sgu_w_out):
    batch, seq, d = x.shape
    assert d == D_MODEL and seq % MOBA_BLOCK == 0 and seq % SGU_CHUNK == 0
    xt = x.reshape(batch * seq, d)
    ffn1_f32 = (ffn1_w_gate, ffn1_w_up, ffn1_w_down)
    ffn2_f32 = (ffn2_w_gate, ffn2_w_up, ffn2_w_down)
    n_f = pl.cdiv(D_FF, FFN_TF)

    def col_blocked(w):
        w = jnp.pad(w.astype(BF16), ((0, 0), (0, n_f * FFN_TF - D_FF)))
        return w.reshape(d, n_f, FFN_TF).transpose(1, 0, 2)

    w_ffn = (col_blocked(ffn1_w_gate[0]), col_blocked(ffn1_w_up[0]), ffn1_w_down[0].astype(BF16))
    w_qkv, w_attn_out = attn_w_qkv.astype(BF16), attn_w_out.astype(BF16)
    w_sgu_in, w_sgu_out = sgu_w_in.astype(BF16), sgu_w_out.astype(BF16)
    for i in range(DEPTH):
        xt, w_ffn = _ffn(xt, ffn1_norm[i], w_ffn, ffn2_f32, i)
        kind, j = i % N_MIXERS, i // N_MIXERS
        if kind == 0:
            q, k, v = _qkv_proj(xt, mix_norm[i], w_qkv, j, attn_q_gain[j], attn_k_gain[j])
            o = _moba_attention(q, k, v, batch, seq)
            xt = _proj_residual(xt, o, w_attn_out, j)
        elif kind == 1:
            xt = _pool_mixer(xt, mix_norm[i], pool_w_group[j], pool_scale[j], batch, seq)
        else:
            z, ssq = _sgu_in(xt, mix_norm[i], w_sgu_in, j)
            xt = _sgu_out(xt, z, ssq, sgu_v_gain[j], sgu_w_spatial[j], sgu_b_spatial[j], w_sgu_out, j)
        if i + 1 < DEPTH:
            xt, w_ffn = _ffn(xt, ffn2_norm[i], w_ffn, ffn1_f32, i + 1)
        else:
            xt, _ = _ffn(xt, ffn2_norm[i], w_ffn)
    return xt.reshape(batch, seq, d)
```
